```python
import jax, jax.numpy as jnp
from jax import lax
import numpy as np

D_MODEL = 1024
BATCH = 8
SEQ = 4096
DEPTH = 1

EPS = 1e-6
ROPE_THETA = 10000.0
NEG_INF = -1e30

SWA_HEADS = 8
SWA_KV_HEADS = 2
SWA_GROUP = SWA_HEADS // SWA_KV_HEADS
SWA_HEAD_DIM = 64
SWA_WINDOW = 128
SWA_BLOCK = 128

MLA_HEADS = 8
MLA_Q_RANK = 256
MLA_KV_RANK = 128
MLA_NOPE_DIM = 64
MLA_ROPE_DIM = 32
MLA_V_DIM = 64
MLA_QK_DIM = MLA_NOPE_DIM + MLA_ROPE_DIM
MLA_Q_BLOCK = 128

SWA_WIDTH = SWA_HEADS * SWA_HEAD_DIM
MLA_WIDTH = MLA_HEADS * MLA_V_DIM
MIX_WIDTH = SWA_WIDTH + MLA_WIDTH

IN_SIZES = (SWA_HEADS * SWA_HEAD_DIM,
            SWA_KV_HEADS * SWA_HEAD_DIM,
            SWA_KV_HEADS * SWA_HEAD_DIM,
            MLA_Q_RANK,
            MLA_KV_RANK,
            MLA_ROPE_DIM)
IN_WIDTH = int(sum(IN_SIZES))
IN_SPLITS = tuple(int(v) for v in np.cumsum(IN_SIZES)[:-1])

PEER_HEADS = 8
PEER_N_KEYS = 128
PEER_EXPERTS = PEER_N_KEYS * PEER_N_KEYS
PEER_QUERY_DIM = 256
PEER_HALF_DIM = PEER_QUERY_DIM // 2
PEER_TOPK = 16
PEER_TOKEN_BLOCK = 128

N_MOD = 6

kernel_name = "hymba_swa_mla_peer_adaln_encoder"


def rmsnorm(x, gain=None):
    xf = x.astype(jnp.float32)
    y = xf * lax.rsqrt(jnp.mean(xf * xf, axis=-1, keepdims=True) + EPS)
    if gain is not None:
        y = y * gain.astype(jnp.float32)
    return y.astype(x.dtype)


def modulate(h, shift, scale):
    return h * (1.0 + scale[:, None, :]) + shift[:, None, :]


def rope_tables(seq, dim):
    inv = 1.0 / (ROPE_THETA ** (jnp.arange(0, dim, 2, dtype=jnp.float32) / dim))
    ang = jnp.arange(seq, dtype=jnp.float32)[:, None] * inv[None, :]
    return jnp.cos(ang), jnp.sin(ang)


def apply_rope(x, cos, sin):
    xf = x.astype(jnp.float32)
    half = xf.shape[-1] // 2
    x1, x2 = xf[..., :half], xf[..., half:]
    c = cos[None, :, None, :]
    s = sin[None, :, None, :]
    return jnp.concatenate([x1 * c - x2 * s, x2 * c + x1 * s], axis=-1).astype(x.dtype)


def windowed_gqa(q, k, v, sink):
    B, S = q.shape[0], q.shape[1]
    nb = S // SWA_BLOCK
    cos, sin = rope_tables(S, SWA_HEAD_DIM)
    q = apply_rope(q, cos, sin)
    k = apply_rope(k, cos, sin)
    qb = q.reshape(B, nb, SWA_BLOCK, SWA_KV_HEADS, SWA_GROUP, SWA_HEAD_DIM)
    pad = ((0, 0), (SWA_BLOCK, SWA_BLOCK), (0, 0), (0, 0))
    kp = jnp.pad(k, pad).reshape(B, nb + 2, SWA_BLOCK, SWA_KV_HEADS, SWA_HEAD_DIM)
    vp = jnp.pad(v, pad).reshape(B, nb + 2, SWA_BLOCK, SWA_KV_HEADS, SWA_HEAD_DIM)
    kb = jnp.concatenate([kp[:, :-2], kp[:, 1:-1], kp[:, 2:]], axis=2)
    vb = jnp.concatenate([vp[:, :-2], vp[:, 1:-1], vp[:, 2:]], axis=2)
    scores = jnp.einsum('bnqhgd,bnkhd->bnhgqk', qb, kb).astype(jnp.float32) * (SWA_HEAD_DIM ** -0.5)
    q_off = jnp.arange(SWA_BLOCK)[:, None]
    k_off = jnp.arange(3 * SWA_BLOCK)[None, :] - SWA_BLOCK
    band = jnp.abs(k_off - q_off) <= SWA_WINDOW
    k_pos = (jnp.arange(nb) * SWA_BLOCK)[:, None] + k_off
    valid = (k_pos >= 0) & (k_pos < S)
    mask = band[None, :, :] & valid[:, None, :]
    scores = jnp.where(mask[None, :, None, None], scores, NEG_INF)
    sink_l = jnp.broadcast_to(sink.astype(jnp.float32).reshape(1, 1, SWA_KV_HEADS, SWA_GROUP, 1, 1),
                              scores.shape[:-1] + (1,))
    probs = jax.nn.softmax(jnp.concatenate([scores, sink_l], axis=-1), axis=-1)[..., :-1]
    out = jnp.einsum('bnhgqk,bnkhd->bnqhgd', probs.astype(v.dtype), vb)
    return out.reshape(B, S, SWA_WIDTH)


def latent_attention(q_lat, kv_lat, k_rope_raw, q_norm, w_q_up, kv_norm, w_kv_up):
    B, S = q_lat.shape[0], q_lat.shape[1]
    nb = S // MLA_Q_BLOCK
    cos, sin = rope_tables(S, MLA_ROPE_DIM)
    q = (rmsnorm(q_lat, q_norm) @ w_q_up).reshape(B, S, MLA_HEADS, MLA_QK_DIM)
    q_nope = q[..., :MLA_NOPE_DIM]
    q_rope = apply_rope(q[..., MLA_NOPE_DIM:], cos, sin)
    kv = (rmsnorm(kv_lat, kv_norm) @ w_kv_up).reshape(B, S, MLA_HEADS, MLA_NOPE_DIM + MLA_V_DIM)
    k_nope = kv[..., :MLA_NOPE_DIM]
    v = kv[..., MLA_NOPE_DIM:]
    k_rope = apply_rope(k_rope_raw[:, :, None, :], cos, sin)[:, :, 0, :]
    scale = MLA_QK_DIM ** -0.5
    qn_blocks = q_nope.reshape(B, nb, MLA_Q_BLOCK, MLA_HEADS, MLA_NOPE_DIM).transpose(1, 0, 2, 3, 4)
    qr_blocks = q_rope.reshape(B, nb, MLA_Q_BLOCK, MLA_HEADS, MLA_ROPE_DIM).transpose(1, 0, 2, 3, 4)

    def attend(blk):
        qn, qr = blk
        s = (jnp.einsum('bqhd,bkhd->bhqk', qn, k_nope)
             + jnp.einsum('bqhd,bkd->bhqk', qr, k_rope)).astype(jnp.float32) * scale
        p = jax.nn.softmax(s, axis=-1).astype(v.dtype)
        return jnp.einsum('bhqk,bkhd->bqhd', p, v)

    out = lax.map(attend, (qn_blocks, qr_blocks))
    return out.transpose(1, 0, 2, 3, 4).reshape(B, S, MLA_WIDTH)


def peer(h, w_query, sub_keys, expert_u, expert_v):
    B, S, D = h.shape
    blocks = h.reshape((B * S) // PEER_TOKEN_BLOCK, PEER_TOKEN_BLOCK, D)

    def block(xb):
        q = (xb @ w_query).reshape(PEER_TOKEN_BLOCK, PEER_HEADS, 2, PEER_HALF_DIM)
        s = jnp.einsum('chpd,hpnd->chpn', q, sub_keys).astype(jnp.float32)
        top_s, top_i = lax.top_k(s, PEER_TOPK)
        cand_s = (top_s[:, :, 0, :, None] + top_s[:, :, 1, None, :]).reshape(
            PEER_TOKEN_BLOCK, PEER_HEADS, PEER_TOPK * PEER_TOPK)
        cand_i = (top_i[:, :, 0, :, None] * PEER_N_KEYS + top_i[:, :, 1, None, :]).reshape(
            PEER_TOKEN_BLOCK, PEER_HEADS, PEER_TOPK * PEER_TOPK)
        best_s, pos = lax.top_k(cand_s, PEER_TOPK)
        idx = jnp.take_along_axis(cand_i, pos, axis=-1)
        g = jax.nn.softmax(best_s, axis=-1)
        u = expert_u[idx]
        act = jax.nn.gelu(jnp.einsum('cd,chkd->chk', xb, u).astype(jnp.float32), approximate=False)
        w = (g * act).astype(xb.dtype)
        return jnp.einsum('chk,chkd->cd', w, expert_v[idx])

    return lax.map(block, blocks).reshape(B, S, D)


def setup_inputs(seed: int = 0) -> dict:
    key = jax.random.key(seed)
    ks = jax.random.split(key, 20)
    f32 = jnp.float32
    nrm = lambda k, shape, s: jax.random.normal(k, shape, f32) * s
    L, D = DEPTH, D_MODEL
    return {
        "x": nrm(ks[0], (BATCH, SEQ, D), 1.0),
        "c": nrm(ks[1], (BATCH, D), 1.0),
        "w_ada": nrm(ks[2], (L, D, N_MOD * D), D ** -0.5),
        "b_ada": nrm(ks[3], (L, N_MOD * D), 0.01),
        "w_in": nrm(ks[4], (L, D, IN_WIDTH), D ** -0.5),
        "swa_sink": nrm(ks[5], (L, SWA_HEADS), 0.5),
        "mla_q_norm": 1.0 + nrm(ks[6], (L, MLA_Q_RANK), 0.02),
        "w_mla_q_up": nrm(ks[7], (L, MLA_Q_RANK, MLA_HEADS * MLA_QK_DIM), MLA_Q_RANK ** -0.5),
        "mla_kv_norm": 1.0 + nrm(ks[8], (L, MLA_KV_RANK), 0.02),
        "w_mla_kv_up": nrm(ks[9], (L, MLA_KV_RANK, MLA_HEADS * (MLA_NOPE_DIM + MLA_V_DIM)), MLA_KV_RANK ** -0.5),
        "out_norm_swa": 1.0 + nrm(ks[10], (L, SWA_WIDTH), 0.02),
        "out_norm_mla": 1.0 + nrm(ks[11], (L, MLA_WIDTH), 0.02),
        "w_out": nrm(ks[12], (L, MIX_WIDTH, D), MIX_WIDTH ** -0.5),
        "w_peer_query": nrm(ks[13], (L, D, PEER_HEADS * PEER_QUERY_DIM), D ** -0.5),
        "peer_sub_keys": nrm(ks[14], (L, PEER_HEADS, 2, PEER_N_KEYS, PEER_HALF_DIM), PEER_HALF_DIM ** -0.5),
        "peer_expert_u": nrm(ks[15], (L, PEER_EXPERTS, D), D ** -0.5),
        "peer_expert_v": nrm(ks[16], (L, PEER_EXPERTS, D), PEER_HEADS ** -0.5),
        "final_norm": 1.0 + nrm(ks[17], (D,), 0.02),
    }


def reference(x, c, w_ada, b_ada, w_in, swa_sink, mla_q_norm, w_mla_q_up, mla_kv_norm, w_mla_kv_up,
              out_norm_swa, out_norm_mla, w_out, w_peer_query, peer_sub_keys, peer_expert_u,
              peer_expert_v, final_norm):
    B, S, D = x.shape
    c_act = jax.nn.silu(c)
    for l in range(DEPTH):
        mod = c_act @ w_ada[l] + b_ada[l]
        sh_a, sc_a, g_a, sh_f, sc_f, g_f = jnp.split(mod, N_MOD, axis=-1)

        h = modulate(rmsnorm(x), sh_a, sc_a)
        proj = h @ w_in[l]
        qa, ka, va, q_lat, kv_lat, k_rope_raw = jnp.split(proj, IN_SPLITS, axis=-1)
        o_a = windowed_gqa(qa.reshape(B, S, SWA_HEADS, SWA_HEAD_DIM),
                           ka.reshape(B, S, SWA_KV_HEADS, SWA_HEAD_DIM),
                           va.reshape(B, S, SWA_KV_HEADS, SWA_HEAD_DIM),
                           swa_sink[l])
        o_b = latent_attention(q_lat, kv_lat, k_rope_raw, mla_q_norm[l], w_mla_q_up[l],
                               mla_kv_norm[l], w_mla_kv_up[l])
        mixed = jnp.concatenate([rmsnorm(o_a, out_norm_swa[l]), rmsnorm(o_b, out_norm_mla[l])], axis=-1)
        x = x + g_a[:, None, :] * (mixed @ w_out[l])

        h = modulate(rmsnorm(x), sh_f, sc_f)
        x = x + g_f[:, None, :] * peer(h, w_peer_query[l], peer_sub_keys[l], peer_expert_u[l], peer_expert_v[l])
    return rmsnorm(x, final_norm)
```

```python
import functools

import numpy as np
import jax
import jax.numpy as jnp
from jax import lax
from jax.experimental import pallas as pl
from jax.experimental.pallas import tpu as pltpu

F32 = jnp.float32
BF16 = jnp.bfloat16

EPS = 1e-6
ROPE_THETA = 10000.0
NEG_INF = -1e30

SWA_HEADS = 8
SWA_KV_HEADS = 2
SWA_HEAD_DIM = 64
SWA_WINDOW = 128

MLA_HEADS = 8
MLA_Q_RANK = 256
MLA_KV_RANK = 128
MLA_NOPE_DIM = 64
MLA_ROPE_DIM = 32
MLA_V_DIM = 64
MLA_QK_DIM = MLA_NOPE_DIM + MLA_ROPE_DIM

SWA_WIDTH = SWA_HEADS * SWA_HEAD_DIM
MLA_WIDTH = MLA_HEADS * MLA_V_DIM

PEER_HEADS = 8
PEER_N_KEYS = 128
PEER_EXPERTS = PEER_N_KEYS * PEER_N_KEYS
PEER_HALF_DIM = 128
PEER_TOPK = 16

N_MOD = 6
LANES = 128
VMEM_LIMIT = 56 * 1024 * 1024

SWA_HEAD_ORDER = (0, 4, 1, 5, 2, 6, 3, 7)

NT_DIMS = (((1,), (1,)), ((), ()))


def _rms(v):
    return v * lax.rsqrt(jnp.mean(v * v, axis=-1, keepdims=True) + EPS)


def _cparams(sem):
    return pltpu.CompilerParams(dimension_semantics=sem, vmem_limit_bytes=VMEM_LIMIT)


def _ada_kernel(c_ref, w_ref, b_ref, o_ref):
    c = c_ref[...]
    ca = c * jax.nn.sigmoid(c)
    o_ref[...] = jnp.dot(ca, w_ref[...], preferred_element_type=F32,
                         precision=lax.Precision.HIGHEST) + b_ref[...]


def _ada(c, w, b):
    bsz, d = c.shape
    n = w.shape[1]
    tn = 1024
    return pl.pallas_call(
        _ada_kernel,
        grid=(n // tn,),
        in_specs=[pl.BlockSpec((bsz, d), lambda j: (0, 0)),
                  pl.BlockSpec((d, tn), lambda j: (0, j)),
                  pl.BlockSpec((1, tn), lambda j: (0, j))],
        out_specs=pl.BlockSpec((bsz, tn), lambda j: (0, j)),
        out_shape=jax.ShapeDtypeStruct((bsz, n), F32),
        compiler_params=_cparams(("arbitrary",)),
        name="ada",
    )(c, w, b.reshape(1, n))


def _rot_partner(v, half, first_half):
    fwd = pltpu.roll(v, LANES - half, 1)
    bwd = pltpu.roll(v, half, 1)
    return jnp.where(first_half, fwd, bwd)


def _inproj_kernel(x_ref, mod_ref, win_ref, qn_ref, wq_ref, kvn_ref, wk_ref, wv_ref,
                   cosa_ref, sina_ref, cosm_ref, sinm_ref,
                   qa_ref, ka_ref, va_ref, qm_ref, km_ref, vm_ref):
    x = x_ref[0]
    sh = mod_ref[0, 0:1, :]
    sc = mod_ref[0, 1:2, :]
    h = (_rms(x) * (1.0 + sc) + sh).astype(BF16)
    proj = jnp.dot(h, win_ref[...], preferred_element_type=F32)

    lane = lax.broadcasted_iota(jnp.int32, (1, LANES), 1)
    cosa, sina = cosa_ref[...], sina_ref[...]
    first_a = (lane % SWA_HEAD_DIM) < (SWA_HEAD_DIM // 2)

    def rope_a(v):
        return v * cosa + _rot_partner(v, SWA_HEAD_DIM // 2, first_a) * sina

    for j in range(SWA_WIDTH // LANES):
        v = proj[:, j * LANES:(j + 1) * LANES]
        qa_ref[0, :, j * LANES:(j + 1) * LANES] = (rope_a(v) * (SWA_HEAD_DIM ** -0.5)).astype(BF16)
    ka_ref[0] = rope_a(proj[:, 512:640]).astype(BF16)
    va_ref[0] = proj[:, 640:768].astype(BF16)

    cosm, sinm = cosm_ref[...], sinm_ref[...]
    first_m = (lane >= MLA_NOPE_DIM) & (lane < MLA_NOPE_DIM + MLA_ROPE_DIM // 2)

    def rope_m(v):
        return v * cosm + _rot_partner(v, MLA_ROPE_DIM // 2, first_m) * sinm

    ql = (_rms(proj[:, 768:1024]) * qn_ref[...]).astype(BF16)
    qup = jnp.dot(ql, wq_ref[...], preferred_element_type=F32)
    kvl = (_rms(proj[:, 1024:1152]) * kvn_ref[...]).astype(BF16)
    kup = jnp.dot(kvl, wk_ref[...], preferred_element_type=F32)
    vm_ref[0] = jnp.dot(kvl, wv_ref[...], preferred_element_type=F32).astype(BF16)
    kr = rope_m(pltpu.roll(proj[:, 1152:1280], MLA_NOPE_DIM, 1))
    for hd in range(MLA_HEADS):
        sl = slice(hd * LANES, (hd + 1) * LANES)
        qm_ref[0, :, sl] = (rope_m(qup[:, sl]) * (MLA_QK_DIM ** -0.5)).astype(BF16)
        km_ref[0, :, sl] = (kup[:, sl] + kr).astype(BF16)


def _inproj(x, mod, win, qn, wq, kvn, wk, wv, cosa, sina, cosm, sinm, ts):
    bsz, s, d = x.shape
    grid = (s // ts, bsz)
    full = lambda a: pl.BlockSpec(a.shape, lambda i, b: (0,) * a.ndim)
    tab = pl.BlockSpec((ts, LANES), lambda i, b: (i, 0))
    tok = lambda w: pl.BlockSpec((1, ts, w), lambda i, b: (b, i, 0))
    widths = (SWA_WIDTH, LANES, LANES, MLA_HEADS * LANES, MLA_HEADS * LANES, MLA_WIDTH)
    return pl.pallas_call(
        _inproj_kernel,
        grid=grid,
        in_specs=[tok(d), pl.BlockSpec((1, N_MOD, d), lambda i, b: (b, 0, 0)),
                  full(win), full(qn), full(wq), full(kvn), full(wk), full(wv),
                  tab, tab, tab, tab],
        out_specs=[tok(w) for w in widths],
        out_shape=[jax.ShapeDtypeStruct((bsz, s, w), BF16) for w in widths],
        compiler_params=_cparams(("arbitrary", "arbitrary")),
        name="inproj",
    )(x, mod, win, qn, wq, kvn, wk, wv, cosa, sina, cosm, sinm)


def _swa_kernel(sink_ref, q_ref, kp_ref, kc_ref, kn_ref, vp_ref, vc_ref, vn_ref, o_ref, *, tq, seq):
    i = pl.program_id(1)
    nk = tq + 2 * SWA_WINDOW
    k = jnp.concatenate([kp_ref[0], kc_ref[0], kn_ref[0]], axis=0)
    v = jnp.concatenate([vp_ref[0], vc_ref[0], vn_ref[0]], axis=0)
    qpos = i * tq + lax.broadcasted_iota(jnp.int32, (tq, nk), 0)
    kpos = i * tq - SWA_WINDOW + lax.broadcasted_iota(jnp.int32, (tq, nk), 1)
    mask = (jnp.abs(qpos - kpos) <= SWA_WINDOW) & (kpos >= 0) & (kpos < seq)
    lo = lax.broadcasted_iota(jnp.int32, (1, LANES), 1) < SWA_HEAD_DIM
    for j in range(SWA_WIDTH // LANES):
        qv = q_ref[0, :, j * LANES:(j + 1) * LANES]
        halves = []
        for half in range(2):
            qh = jnp.where(lo if half == 0 else jnp.logical_not(lo), qv, jnp.zeros_like(qv))
            s = lax.dot_general(qh, k, NT_DIMS, preferred_element_type=F32)
            s = jnp.where(mask, s, NEG_INF)
            sink = sink_ref[2 * j + half]
            m = jnp.maximum(jnp.max(s, axis=-1, keepdims=True), sink)
            p = jnp.exp(s - m)
            denom = jnp.sum(p, axis=-1, keepdims=True) + jnp.exp(sink - m)
            r = jnp.dot(p.astype(BF16), v, preferred_element_type=F32)
            halves.append(r / denom)
        o_ref[0, :, j * LANES:(j + 1) * LANES] = jnp.where(lo, halves[0], halves[1])


def _swa(sinks, qa, ka, va, tq):
    bsz, s, _ = qa.shape
    nblk = s // SWA_WINDOW
    r = tq // SWA_WINDOW
    prev = pl.BlockSpec((1, SWA_WINDOW, LANES), lambda b, i: (b, jnp.maximum(i * r - 1, 0), 0))
    cur = pl.BlockSpec((1, tq, LANES), lambda b, i: (b, i, 0))
    nxt = pl.BlockSpec((1, SWA_WINDOW, LANES), lambda b, i: (b, jnp.minimum((i + 1) * r, nblk - 1), 0))
    return pl.pallas_call(
        functools.partial(_swa_kernel, tq=tq, seq=s),
        grid=(bsz, s // tq),
        in_specs=[pl.BlockSpec(memory_space=pltpu.SMEM),
                  pl.BlockSpec((1, tq, SWA_WIDTH), lambda b, i: (b, i, 0)),
                  prev, cur, nxt, prev, cur, nxt],
        out_specs=pl.BlockSpec((1, tq, SWA_WIDTH), lambda b, i: (b, i, 0)),
        out_shape=jax.ShapeDtypeStruct((bsz, s, SWA_WIDTH), F32),
        compiler_params=_cparams(("arbitrary", "arbitrary")),
        name="swa",
    )(sinks, qa, ka, ka, ka, va, va, va)


def _mla_kernel(q_ref, k_ref, v_ref, o_ref):
    v = v_ref[0]
    lo = lax.broadcasted_iota(jnp.int32, (1, LANES), 1) < MLA_V_DIM
    halves = []
    for half in range(2):
        sl = slice(half * LANES, (half + 1) * LANES)
        s = lax.dot_general(q_ref[0, :, sl], k_ref[0, :, sl], NT_DIMS, preferred_element_type=F32)
        m = jnp.max(s, axis=-1, keepdims=True)
        p = jnp.exp(s - m)
        denom = jnp.sum(p, axis=-1, keepdims=True)
        r = jnp.dot(p.astype(BF16), v, preferred_element_type=F32)
        halves.append(r / denom)
    o_ref[0] = jnp.where(lo, halves[0], halves[1])


def _mla(qm, km, vm, tq):
    bsz, s, _ = qm.shape
    pairs = MLA_HEADS // 2
    return pl.pallas_call(
        _mla_kernel,
        grid=(bsz, pairs, s // tq),
        in_specs=[pl.BlockSpec((1, tq, 2 * LANES), lambda b, p, i: (b, i, p)),
                  pl.BlockSpec((1, s, 2 * LANES), lambda b, p, i: (b, 0, p)),
                  pl.BlockSpec((1, s, LANES), lambda b, p, i: (b, 0, p))],
        out_specs=pl.BlockSpec((1, tq, LANES), lambda b, p, i: (b, i, p)),
        out_shape=jax.ShapeDtypeStruct((bsz, s, MLA_WIDTH), F32),
        compiler_params=_cparams(("arbitrary", "arbitrary", "arbitrary")),
        name="mla",
    )(qm, km, vm)


def _outproj_kernel(oa_ref, ob_ref, x_ref, mod_ref, ga_ref, gb_ref, wo_ref, x1_ref, h2_ref):
    ma = (_rms(oa_ref[0]) * ga_ref[...]).astype(BF16)
    mb = (_rms(ob_ref[0]) * gb_ref[...]).astype(BF16)
    y = (jnp.dot(ma, wo_ref[0:SWA_WIDTH, :], preferred_element_type=F32)
         + jnp.dot(mb, wo_ref[SWA_WIDTH:, :], preferred_element_type=F32))
    x1 = x_ref[0] + mod_ref[0, 2:3, :] * y
    x1_ref[0] = x1
    h2_ref[0] = (_rms(x1) * (1.0 + mod_ref[0, 4:5, :]) + mod_ref[0, 3:4, :]).astype(BF16)


def _outproj(oa, ob, x, mod, ga, gb, wo, ts):
    bsz, s, d = x.shape
    full = lambda a: pl.BlockSpec(a.shape, lambda i, b: (0,) * a.ndim)
    tok = lambda w: pl.BlockSpec((1, ts, w), lambda i, b: (b, i, 0))
    return pl.pallas_call(
        _outproj_kernel,
        grid=(s // ts, bsz),
        in_specs=[tok(SWA_WIDTH), tok(MLA_WIDTH), tok(d),
                  pl.BlockSpec((1, N_MOD, d), lambda i, b: (b, 0, 0)),
                  full(ga), full(gb), full(wo)],
        out_specs=[tok(d), tok(d)],
        out_shape=[jax.ShapeDtypeStruct((bsz, s, d), F32), jax.ShapeDtypeStruct((bsz, s, d), BF16)],
        compiler_params=_cparams(("arbitrary", "arbitrary")),
        name="outproj",
    )(oa, ob, x, mod, ga, gb, wo)


def _topk_rank(s):
    n = s.shape[0]
    key_id = lax.broadcasted_iota(jnp.int32, s.shape, 0)
    slot = lax.broadcasted_iota(jnp.int32, (PEER_TOPK, s.shape[1]), 0)

    def body(it, carry):
        s, rank, vals = carry
        m = jnp.max(s, axis=0, keepdims=True)
        first = jnp.min(jnp.where(s == m, key_id, n), axis=0, keepdims=True)
        sel = key_id == first
        rank = jnp.where(sel, it.astype(F32), rank)
        s = jnp.where(sel, -jnp.inf, s)
        vals = jnp.where(slot == it, m, vals)
        return s, rank, vals

    init = (s, jnp.full(s.shape, float(PEER_TOPK), F32), jnp.zeros((PEER_TOPK, s.shape[1]), F32))
    _, rank, vals = lax.fori_loop(0, PEER_TOPK, body, init)
    return vals, rank


def _staircase(v0, v1):
    row = lax.broadcasted_iota(jnp.int32, v0.shape, 0)
    best0 = v0[0:1] + v1[0:1]

    def body(it, carry):
        cnt, front, z = carry
        m = jnp.max(front, axis=0, keepdims=True)
        first = jnp.min(jnp.where(front == m, row, PEER_TOPK), axis=0, keepdims=True)
        sel = row == first
        z = z + jnp.exp(m - best0)
        cnt = cnt + sel.astype(jnp.int32)
        c_sel = jnp.max(jnp.where(sel, cnt, 0), axis=0, keepdims=True)
        nxt = jnp.full_like(m, -jnp.inf)
        for j in range(1, PEER_TOPK):
            nxt = jnp.where(c_sel == j, v1[j:j + 1], nxt)
        front = jnp.where(sel, v0 + nxt, front)
        return cnt, front, z

    init = (jnp.zeros(v0.shape, jnp.int32), v0 + v1[0:1], jnp.zeros_like(best0))
    cnt, _, z = lax.fori_loop(0, PEER_TOPK, body, init)
    return cnt, z


def _peer_sel_kernel(h_ref, wq_ref, keys_ref, r1_ref, e1_ref, e0_ref, ca_ref, s_scr):
    ct = h_ref.shape[0]
    qt = lax.dot_general(wq_ref[...], h_ref[...], NT_DIMS, preferred_element_type=F32).astype(BF16)
    for hp in range(2 * PEER_HEADS):
        s_scr[hp] = jnp.dot(keys_ref[hp], qt[hp * PEER_HALF_DIM:(hp + 1) * PEER_HALF_DIM, :],
                            preferred_element_type=F32)
    for g in range(ct // LANES):
        cols = slice(g * LANES, (g + 1) * LANES)

        def per_head(hd, carry):
            s0 = s_scr[2 * hd, :, cols]
            s1 = s_scr[2 * hd + 1, :, cols]
            v0, rank0 = _topk_rank(s0)
            v1, rank1 = _topk_rank(s1)
            cnt, z = _staircase(v0, v1)
            cnt_f = cnt.astype(F32)
            ca = jnp.zeros_like(s0)
            for i in range(PEER_TOPK):
                ca = jnp.where(rank0 == float(i), cnt_f[i:i + 1], ca)
            r1_ref[hd, :, cols] = rank1
            e1_ref[hd, :, cols] = jnp.exp(s1 - v1[0:1])
            e0_ref[hd, :, cols] = jnp.exp(s0 - v0[0:1]) / z
            ca_ref[hd, :, cols] = ca
            return carry

        lax.fori_loop(0, PEER_HEADS, per_head, 0)


def _peer_sel(h2, wqt, keys, ct):
    t, d = h2.shape
    full = lambda a: pl.BlockSpec(a.shape, lambda i: (0,) * a.ndim)
    out = pl.BlockSpec((PEER_HEADS, PEER_N_KEYS, ct), lambda i: (0, 0, i))
    shp = jax.ShapeDtypeStruct((PEER_HEADS, PEER_N_KEYS, t), F32)
    return pl.pallas_call(
        _peer_sel_kernel,
        grid=(t // ct,),
        in_specs=[pl.BlockSpec((ct, d), lambda i: (i, 0)), full(wqt), full(keys)],
        out_specs=[out, out, out, out],
        out_shape=[shp, shp, shp, shp],
        scratch_shapes=[pltpu.VMEM((2 * PEER_HEADS, PEER_N_KEYS, ct), F32)],
        compiler_params=_cparams(("arbitrary",)),
        name="peer_sel",
    )(h2, wqt, keys)


def _peer_mix_kernel(h_ref, x1_ref, mod_ref, r1_ref, e1_ref, e0_ref, ca_ref, u_ref, vt_ref, fn_ref,
                     o_ref, acc_ref, *, rows_per_step):
    j = pl.program_id(1)

    @pl.when(j == 0)
    def _():
        acc_ref[...] = jnp.zeros_like(acc_ref)

    st = lax.dot_general(u_ref[...], h_ref[...], NT_DIMS, preferred_element_type=F32)
    ws = []
    for aa in range(rows_per_step):
        a = j * rows_per_step + aa
        gate = jnp.zeros((PEER_N_KEYS, h_ref.shape[0]), F32)
        for hd in range(PEER_HEADS):
            ca = ca_ref[hd, pl.ds(a, 1), :]
            e0 = e0_ref[hd, pl.ds(a, 1), :]
            gate = gate + jnp.where(r1_ref[hd] < ca, e1_ref[hd] * e0, 0.0)
        sc = st[aa * PEER_N_KEYS:(aa + 1) * PEER_N_KEYS, :]
        act = 0.5 * sc * (1.0 + lax.erf(sc * (2.0 ** -0.5)))
        ws.append((gate * act).astype(BF16))
    w = jnp.concatenate(ws, axis=0)
    acc_ref[...] += jnp.dot(vt_ref[...], w, preferred_element_type=F32)

    @pl.when(j == pl.num_programs(1) - 1)
    def _():
        x2 = x1_ref[...] + mod_ref[0, 5:6, :] * acc_ref[...].T
        o_ref[...] = _rms(x2) * fn_ref[...]


def _peer_mix(h2, x1, mod, r1, e1, e0, ca, u, vt, fn, ct, ec, seq):
    t, d = h2.shape
    rows = ec // PEER_N_KEYS
    sel = pl.BlockSpec((PEER_HEADS, PEER_N_KEYS, ct), lambda i, j: (0, 0, i))
    tok = pl.BlockSpec((ct, d), lambda i, j: (i, 0))
    return pl.pallas_call(
        functools.partial(_peer_mix_kernel, rows_per_step=rows),
        grid=(t // ct, PEER_EXPERTS // ec),
        in_specs=[tok, tok,
                  pl.BlockSpec((1, N_MOD, d), lambda i, j: ((i * ct) // seq, 0, 0)),
                  sel, sel, sel, sel,
                  pl.BlockSpec((ec, d), lambda i, j: (j, 0)),
                  pl.BlockSpec((d, ec), lambda i, j: (0, j)),
                  pl.BlockSpec((1, d), lambda i, j: (0, 0))],
        out_specs=tok,
        out_shape=jax.ShapeDtypeStruct((t, d), F32),
        scratch_shapes=[pltpu.VMEM((d, ct), F32)],
        compiler_params=_cparams(("arbitrary", "arbitrary")),
        name="peer_mix",
    )(h2, x1, mod, r1, e1, e0, ca, u, vt, fn)


def _rope_tables(seq):
    pos = jnp.arange(seq, dtype=F32)[:, None]

    def cs(dim):
        inv = 1.0 / (ROPE_THETA ** (jnp.arange(0, dim, 2, dtype=F32) / dim))
        ang = pos * inv[None, :]
        return jnp.cos(ang), jnp.sin(ang)

    ca, sa = cs(SWA_HEAD_DIM)
    cosa = jnp.concatenate([ca, ca, ca, ca], axis=1)
    sina = jnp.concatenate([-sa, sa, -sa, sa], axis=1)
    cm, sm = cs(MLA_ROPE_DIM)
    one = jnp.ones((seq, MLA_NOPE_DIM), F32)
    pad1 = jnp.ones((seq, LANES - MLA_QK_DIM), F32)
    cosm = jnp.concatenate([one, cm, cm, pad1], axis=1)
    sinm = jnp.concatenate([0.0 * one, -sm, sm, 0.0 * pad1], axis=1)
    return cosa, sina, cosm, sinm


def _tile(n, pref):
    t = min(n, pref)
    assert n % t == 0, (n, t)
    return t


def kernel(x, c, w_ada, b_ada, w_in, swa_sink, mla_q_norm, w_mla_q_up, mla_kv_norm, w_mla_kv_up,
           out_norm_swa, out_norm_mla, w_out, w_peer_query, peer_sub_keys, peer_expert_u,
           peer_expert_v, final_norm):
    bsz, seq, d = x.shape
    assert w_ada.shape[0] == 1 and seq % SWA_WINDOW == 0 and d == 1024
    qcols = np.concatenate([np.arange(h * SWA_HEAD_DIM, (h + 1) * SWA_HEAD_DIM) for h in SWA_HEAD_ORDER])
    cosa, sina, cosm, sinm = _rope_tables(seq)
    ts = _tile(seq, 512)
    tq_swa = _tile(seq, 256)
    tq_mla = _tile(seq, 256)
    ct_sel = _tile(bsz * seq, 256)
    ct_mix = _tile(seq, 512)

    for l in range(1):
        mod = _ada(c, w_ada[l], b_ada[l]).reshape(bsz, N_MOD, d)

        win = w_in[l]
        pad_cols = jnp.zeros((d, LANES - MLA_ROPE_DIM), win.dtype)
        win_p = jnp.concatenate([win[:, qcols], win[:, SWA_WIDTH:], pad_cols], axis=1).astype(BF16)
        wq_p = jnp.pad(w_mla_q_up[l].reshape(MLA_Q_RANK, MLA_HEADS, MLA_QK_DIM),
                       ((0, 0), (0, 0), (0, LANES - MLA_QK_DIM))).reshape(MLA_Q_RANK, MLA_HEADS * LANES).astype(BF16)
        wkv = w_mla_kv_up[l].reshape(MLA_KV_RANK, MLA_HEADS, MLA_NOPE_DIM + MLA_V_DIM)
        wk_p = jnp.pad(wkv[:, :, :MLA_NOPE_DIM], ((0, 0), (0, 0), (0, LANES - MLA_NOPE_DIM))
                       ).reshape(MLA_KV_RANK, MLA_HEADS * LANES).astype(BF16)
        wv_p = wkv[:, :, MLA_NOPE_DIM:].reshape(MLA_KV_RANK, MLA_WIDTH).astype(BF16)

        qa, ka, va, qm, km, vm = _inproj(
            x, mod, win_p, mla_q_norm[l].reshape(1, -1), wq_p, mla_kv_norm[l].reshape(1, -1), wk_p, wv_p,
            cosa, sina, cosm, sinm, ts)

        oa = _swa(swa_sink[l][np.array(SWA_HEAD_ORDER)], qa, ka, va, tq_swa)
        ob = _mla(qm, km, vm, tq_mla)

        wo = w_out[l]
        wo_p = jnp.concatenate([wo[qcols, :], wo[SWA_WIDTH:, :]], axis=0).astype(BF16)
        x1, h2 = _outproj(oa, ob, x, mod, out_norm_swa[l][qcols].reshape(1, -1),
                          out_norm_mla[l].reshape(1, -1), wo_p, ts)

        h2f = h2.reshape(bsz * seq, d)
        wqt = w_peer_query[l].T.astype(BF16)
        keys = peer_sub_keys[l].reshape(2 * PEER_HEADS, PEER_N_KEYS, PEER_HALF_DIM).astype(BF16)
        r1, e1, e0, ca = _peer_sel(h2f, wqt, keys, ct_sel)

        u = peer_expert_u[l].astype(BF16)
        vt = peer_expert_v[l].T.astype(BF16)
        fn = final_norm.reshape(1, d)
        x = _peer_mix(h2f, x1.reshape(bsz * seq, d), mod, r1, e1, e0, ca, u, vt, fn,
                      ct_mix, 512, seq).reshape(bsz, seq, d)
    return x
```

```python
import functools

import numpy as np
import jax
import jax.numpy as jnp
from jax import lax
from jax.experimental import pallas as pl
from jax.experimental.pallas import tpu as pltpu

F32 = jnp.float32
BF16 = jnp.bfloat16

EPS = 1e-6
ROPE_THETA = 10000.0
NEG_INF = -1e30

SWA_HEADS = 8
SWA_KV_HEADS = 2
SWA_HEAD_DIM = 64
SWA_WINDOW = 128

MLA_HEADS = 8
MLA_Q_RANK = 256
MLA_KV_RANK = 128
MLA_NOPE_DIM = 64
MLA_ROPE_DIM = 32
MLA_V_DIM = 64
MLA_QK_DIM = MLA_NOPE_DIM + MLA_ROPE_DIM

SWA_WIDTH = SWA_HEADS * SWA_HEAD_DIM
MLA_WIDTH = MLA_HEADS * MLA_V_DIM

PEER_HEADS = 8
PEER_N_KEYS = 128
PEER_EXPERTS = PEER_N_KEYS * PEER_N_KEYS
PEER_HALF_DIM = 128
PEER_TOPK = 16

N_MOD = 6
LANES = 128
VMEM_LIMIT = 56 * 1024 * 1024

SWA_HEAD_ORDER = (0, 4, 1, 5, 2, 6, 3, 7)

NT_DIMS = (((1,), (1,)), ((), ()))


def _rms(v):
    return v * lax.rsqrt(jnp.mean(v * v, axis=-1, keepdims=True) + EPS)


def _cparams(sem):
    return pltpu.CompilerParams(dimension_semantics=sem, vmem_limit_bytes=VMEM_LIMIT)


def _ada_kernel(c_ref, w_ref, b_ref, o_ref):
    c = c_ref[...]
    ca = c * jax.nn.sigmoid(c)
    o_ref[...] = jnp.dot(ca, w_ref[...], preferred_element_type=F32,
                         precision=lax.Precision.HIGHEST) + b_ref[...]


def _ada(c, w, b):
    bsz, d = c.shape
    n = w.shape[1]
    tn = 1024
    return pl.pallas_call(
        _ada_kernel,
        grid=(n // tn,),
        in_specs=[pl.BlockSpec((bsz, d), lambda j: (0, 0)),
                  pl.BlockSpec((d, tn), lambda j: (0, j)),
                  pl.BlockSpec((1, tn), lambda j: (0, j))],
        out_specs=pl.BlockSpec((bsz, tn), lambda j: (0, j)),
        out_shape=jax.ShapeDtypeStruct((bsz, n), F32),
        compiler_params=_cparams(("arbitrary",)),
        name="ada",
    )(c, w, b.reshape(1, n))


def _rot_partner(v, half, first_half):
    fwd = pltpu.roll(v, LANES - half, 1)
    bwd = pltpu.roll(v, half, 1)
    return jnp.where(first_half, fwd, bwd)


def _inproj_kernel(x_ref, mod_ref, win_ref, qn_ref, wq_ref, kvn_ref, wk_ref, wv_ref,
                   cosa_ref, sina_ref, cosm_ref, sinm_ref,
                   qa_ref, ka_ref, va_ref, qm_ref, km_ref, vm_ref):
    x = x_ref[0]
    sh = mod_ref[0, 0:1, :]
    sc = mod_ref[0, 1:2, :]
    h = (_rms(x) * (1.0 + sc) + sh).astype(BF16)
    proj = jnp.dot(h, win_ref[...], preferred_element_type=F32)

    lane = lax.broadcasted_iota(jnp.int32, (1, LANES), 1)
    cosa, sina = cosa_ref[...], sina_ref[...]
    first_a = (lane % SWA_HEAD_DIM) < (SWA_HEAD_DIM // 2)

    def rope_a(v):
        return v * cosa + _rot_partner(v, SWA_HEAD_DIM // 2, first_a) * sina

    for j in range(SWA_WIDTH // LANES):
        v = proj[:, j * LANES:(j + 1) * LANES]
        qa_ref[0, :, j * LANES:(j + 1) * LANES] = (rope_a(v) * (SWA_HEAD_DIM ** -0.5)).astype(BF16)
    ka_ref[0] = rope_a(proj[:, 512:640]).astype(BF16)
    va_ref[0] = proj[:, 640:768].astype(BF16)

    cosm, sinm = cosm_ref[...], sinm_ref[...]
    first_m = (lane >= MLA_NOPE_DIM) & (lane < MLA_NOPE_DIM + MLA_ROPE_DIM // 2)

    def rope_m(v):
        return v * cosm + _rot_partner(v, MLA_ROPE_DIM // 2, first_m) * sinm

    ql = (_rms(proj[:, 768:1024]) * qn_ref[...]).astype(BF16)
    qup = jnp.dot(ql, wq_ref[...], preferred_element_type=F32)
    kvl = (_rms(proj[:, 1024:1152]) * kvn_ref[...]).astype(BF16)
    kup = jnp.dot(kvl, wk_ref[...], preferred_element_type=F32)
    vm_ref[0] = jnp.dot(kvl, wv_ref[...], preferred_element_type=F32).astype(BF16)
    kr = rope_m(pltpu.roll(proj[:, 1152:1280], MLA_NOPE_DIM, 1))
    for hd in range(MLA_HEADS):
        sl = slice(hd * LANES, (hd + 1) * LANES)
        qm_ref[0, :, sl] = (rope_m(qup[:, sl]) * (MLA_QK_DIM ** -0.5)).astype(BF16)
        km_ref[0, :, sl] = (kup[:, sl] + kr).astype(BF16)


def _inproj(x, mod, win, qn, wq, kvn, wk, wv, cosa, sina, cosm, sinm, ts):
    bsz, s, d = x.shape
    grid = (s // ts, bsz)
    full = lambda a: pl.BlockSpec(a.shape, lambda i, b: (0,) * a.ndim)
    tab = pl.BlockSpec((ts, LANES), lambda i, b: (i, 0))
    tok = lambda w: pl.BlockSpec((1, ts, w), lambda i, b: (b, i, 0))
    widths = (SWA_WIDTH, LANES, LANES, MLA_HEADS * LANES, MLA_HEADS * LANES, MLA_WIDTH)
    return pl.pallas_call(
        _inproj_kernel,
        grid=grid,
        in_specs=[tok(d), pl.BlockSpec((1, N_MOD, d), lambda i, b: (b, 0, 0)),
                  full(win), full(qn), full(wq), full(kvn), full(wk), full(wv),
                  tab, tab, tab, tab],
        out_specs=[tok(w) for w in widths],
        out_shape=[jax.ShapeDtypeStruct((bsz, s, w), BF16) for w in widths],
        compiler_params=_cparams(("arbitrary", "arbitrary")),
        name="inproj",
    )(x, mod, win, qn, wq, kvn, wk, wv, cosa, sina, cosm, sinm)


def _swa_kernel(sink_ref, q_ref, kp_ref, kc_ref, kn_ref, vp_ref, vc_ref, vn_ref, o_ref, *, tq, seq):
    i = pl.program_id(1)
    nk = tq + 2 * SWA_WINDOW
    k = jnp.concatenate([kp_ref[0], kc_ref[0], kn_ref[0]], axis=0)
    v = jnp.concatenate([vp_ref[0], vc_ref[0], vn_ref[0]], axis=0)
    qpos = i * tq + lax.broadcasted_iota(jnp.int32, (tq, nk), 0)
    kpos = i * tq - SWA_WINDOW + lax.broadcasted_iota(jnp.int32, (tq, nk), 1)
    mask = (jnp.abs(qpos - kpos) <= SWA_WINDOW) & (kpos >= 0) & (kpos < seq)
    lo = lax.broadcasted_iota(jnp.int32, (1, LANES), 1) < SWA_HEAD_DIM
    for j in range(SWA_WIDTH // LANES):
        qv = q_ref[0, :, j * LANES:(j + 1) * LANES]
        halves = []
        for half in range(2):
            qh = jnp.where(lo if half == 0 else jnp.logical_not(lo), qv, jnp.zeros_like(qv))
            s = lax.dot_general(qh, k, NT_DIMS, preferred_element_type=F32)
            s = jnp.where(mask, s, NEG_INF)
            sink = sink_ref[2 * j + half]
            m = jnp.maximum(jnp.max(s, axis=-1, keepdims=True), sink)
            p = jnp.exp(s - m)
            denom = jnp.sum(p, axis=-1, keepdims=True) + jnp.exp(sink - m)
            r = jnp.dot(p.astype(BF16), v, preferred_element_type=F32)
            halves.append(r / denom)
        o_ref[0, :, j * LANES:(j + 1) * LANES] = jnp.where(lo, halves[0], halves[1])


def _swa(sinks, qa, ka, va, tq):
    bsz, s, _ = qa.shape
    nblk = s // SWA_WINDOW
    r = tq // SWA_WINDOW
    prev = pl.BlockSpec((1, SWA_WINDOW, LANES), lambda b, i: (b, jnp.maximum(i * r - 1, 0), 0))
    cur = pl.BlockSpec((1, tq, LANES), lambda b, i: (b, i, 0))
    nxt = pl.BlockSpec((1, SWA_WINDOW, LANES), lambda b, i: (b, jnp.minimum((i + 1) * r, nblk - 1), 0))
    return pl.pallas_call(
        functools.partial(_swa_kernel, tq=tq, seq=s),
        grid=(bsz, s // tq),
        in_specs=[pl.BlockSpec(memory_space=pltpu.SMEM),
                  pl.BlockSpec((1, tq, SWA_WIDTH), lambda b, i: (b, i, 0)),
                  prev, cur, nxt, prev, cur, nxt],
        out_specs=pl.BlockSpec((1, tq, SWA_WIDTH), lambda b, i: (b, i, 0)),
        out_shape=jax.ShapeDtypeStruct((bsz, s, SWA_WIDTH), F32),
        compiler_params=_cparams(("arbitrary", "arbitrary")),
        name="swa",
    )(sinks, qa, ka, ka, ka, va, va, va)


def _mla_kernel(q_ref, k_ref, v_ref, o_ref):
    v = v_ref[0]
    lo = lax.broadcasted_iota(jnp.int32, (1, LANES), 1) < MLA_V_DIM
    halves = []
    for half in range(2):
        sl = slice(half * LANES, (half + 1) * LANES)
        s = lax.dot_general(q_ref[0, :, sl], k_ref[0, :, sl], NT_DIMS, preferred_element_type=F32)
        m = jnp.max(s, axis=-1, keepdims=True)
        p = jnp.exp(s - m)
        denom = jnp.sum(p, axis=-1, keepdims=True)
        r = jnp.dot(p.astype(BF16), v, preferred_element_type=F32)
        halves.append(r / denom)
    o_ref[0] = jnp.where(lo, halves[0], halves[1])


def _mla(qm, km, vm, tq):
    bsz, s, _ = qm.shape
    pairs = MLA_HEADS // 2
    return pl.pallas_call(
        _mla_kernel,
        grid=(bsz, pairs, s // tq),
        in_specs=[pl.BlockSpec((1, tq, 2 * LANES), lambda b, p, i: (b, i, p)),
                  pl.BlockSpec((1, s, 2 * LANES), lambda b, p, i: (b, 0, p)),
                  pl.BlockSpec((1, s, LANES), lambda b, p, i: (b, 0, p))],
        out_specs=pl.BlockSpec((1, tq, LANES), lambda b, p, i: (b, i, p)),
        out_shape=jax.ShapeDtypeStruct((bsz, s, MLA_WIDTH), F32),
        compiler_params=_cparams(("arbitrary", "arbitrary", "arbitrary")),
        name="mla",
    )(qm, km, vm)


def _outproj_kernel(oa_ref, ob_ref, x_ref, mod_ref, ga_ref, gb_ref, wo_ref, x1_ref, h2_ref):
    ma = (_rms(oa_ref[0]) * ga_ref[...]).astype(BF16)
    mb = (_rms(ob_ref[0]) * gb_ref[...]).astype(BF16)
    y = (jnp.dot(ma, wo_ref[0:SWA_WIDTH, :], preferred_element_type=F32)
         + jnp.dot(mb, wo_ref[SWA_WIDTH:, :], preferred_element_type=F32))
    x1 = x_ref[0] + mod_ref[0, 2:3, :] * y
    x1_ref[0] = x1
    h2_ref[0] = (_rms(x1) * (1.0 + mod_ref[0, 4:5, :]) + mod_ref[0, 3:4, :]).astype(BF16)


def _outproj(oa, ob, x, mod, ga, gb, wo, ts):
    bsz, s, d = x.shape
    full = lambda a: pl.BlockSpec(a.shape, lambda i, b: (0,) * a.ndim)
    tok = lambda w: pl.BlockSpec((1, ts, w), lambda i, b: (b, i, 0))
    return pl.pallas_call(
        _outproj_kernel,
        grid=(s // ts, bsz),
        in_specs=[tok(SWA_WIDTH), tok(MLA_WIDTH), tok(d),
                  pl.BlockSpec((1, N_MOD, d), lambda i, b: (b, 0, 0)),
                  full(ga), full(gb), full(wo)],
        out_specs=[tok(d), tok(d)],
        out_shape=[jax.ShapeDtypeStruct((bsz, s, d), F32), jax.ShapeDtypeStruct((bsz, s, d), BF16)],
        compiler_params=_cparams(("arbitrary", "arbitrary")),
        name="outproj",
    )(oa, ob, x, mod, ga, gb, wo)


def _topk_rank(s):
    n = s.shape[0]
    key_id = lax.broadcasted_iota(jnp.int32, s.shape, 0)
    slot = lax.broadcasted_iota(jnp.int32, (PEER_TOPK, s.shape[1]), 0)

    def body(it, carry):
        s, rank, vals = carry
        m = jnp.max(s, axis=0, keepdims=True)
        first = jnp.min(jnp.where(s == m, key_id, n), axis=0, keepdims=True)
        sel = key_id == first
        rank = jnp.where(sel, it.astype(F32), rank)
        s = jnp.where(sel, -jnp.inf, s)
        vals = jnp.where(slot == it, m, vals)
        return s, rank, vals

    init = (s, jnp.full(s.shape, float(PEER_TOPK), F32), jnp.zeros((PEER_TOPK, s.shape[1]), F32))
    _, rank, vals = lax.fori_loop(0, PEER_TOPK, body, init)
    return vals, rank


def _staircase(v0, v1):
    row = lax.broadcasted_iota(jnp.int32, v0.shape, 0)
    best0 = v0[0:1] + v1[0:1]

    def body(it, carry):
        cnt, front, z = carry
        m = jnp.max(front, axis=0, keepdims=True)
        first = jnp.min(jnp.where(front == m, row, PEER_TOPK), axis=0, keepdims=True)
        sel = row == first
        z = z + jnp.exp(m - best0)
        cnt = cnt + sel.astype(jnp.int32)
        c_sel = jnp.max(jnp.where(sel, cnt, 0), axis=0, keepdims=True)
        nxt = jnp.full_like(m, -jnp.inf)
        for j in range(1, PEER_TOPK):
            nxt = jnp.where(c_sel == j, v1[j:j + 1], nxt)
        front = jnp.where(sel, v0 + nxt, front)
        return cnt, front, z

    init = (jnp.zeros(v0.shape, jnp.int32), v0 + v1[0:1], jnp.zeros_like(best0))
    cnt, _, z = lax.fori_loop(0, PEER_TOPK, body, init)
    return cnt, z


def _peer_sel_kernel(h_ref, wq_ref, keys_ref, r1_ref, e1_ref, e0_ref, ca_ref, s_scr):
    ct = h_ref.shape[0]
    qt = lax.dot_general(wq_ref[...], h_ref[...], NT_DIMS, preferred_element_type=F32).astype(BF16)
    for hp in range(2 * PEER_HEADS):
        s_scr[hp] = jnp.dot(keys_ref[hp], qt[hp * PEER_HALF_DIM:(hp + 1) * PEER_HALF_DIM, :],
                            preferred_element_type=F32)
    for g in range(ct // LANES):
        cols = slice(g * LANES, (g + 1) * LANES)

        def per_head(hd, carry):
            s0 = s_scr[2 * hd, :, cols]
            s1 = s_scr[2 * hd + 1, :, cols]
            v0, rank0 = _topk_rank(s0)
            v1, rank1 = _topk_rank(s1)
            cnt, z = _staircase(v0, v1)
            cnt_f = cnt.astype(F32)
            ca = jnp.zeros_like(s0)
            for i in range(PEER_TOPK):
                ca = jnp.where(rank0 == float(i), cnt_f[i:i + 1], ca)
            r1_ref[g, hd] = rank1
            e1_ref[g, hd] = jnp.exp(s1 - v1[0:1])
            e0_ref[g, hd] = jnp.exp(s0 - v0[0:1]) / z
            ca_ref[g, hd] = ca
            return carry

        lax.fori_loop(0, PEER_HEADS, per_head, 0)


def _peer_sel(h2, wqt, keys, ct):
    t, d = h2.shape
    full = lambda a: pl.BlockSpec(a.shape, lambda i: (0,) * a.ndim)
    out = pl.BlockSpec((ct // LANES, PEER_HEADS, PEER_N_KEYS, LANES), lambda i: (i, 0, 0, 0))
    shp = jax.ShapeDtypeStruct((t // LANES, PEER_HEADS, PEER_N_KEYS, LANES), F32)
    return pl.pallas_call(
        _peer_sel_kernel,
        grid=(t // ct,),
        in_specs=[pl.BlockSpec((ct, d), lambda i: (i, 0)), full(wqt), full(keys)],
        out_specs=[out, out, out, out],
        out_shape=[shp, shp, shp, shp],
        scratch_shapes=[pltpu.VMEM((2 * PEER_HEADS, PEER_N_KEYS, ct), F32)],
        compiler_params=_cparams(("arbitrary",)),
        name="peer_sel",
    )(h2, wqt, keys)


def _peer_mix_kernel(h_ref, x1_ref, mod_ref, r1_ref, e1_ref, e0_ref, ca_ref, u_ref, vt_ref, fn_ref,
                     o_ref, acc_ref, st_ref, w_ref, *, rows_per_step, n_chunks, n_steps):
    s = pl.program_id(0)
    slot = s % 2
    prev = 1 - slot
    chunk_b = jnp.clip(s - 1, 0, n_steps - 1) % n_chunks
    chunk_c = jnp.clip(s - 2, 0, n_steps - 1) % n_chunks

    @pl.when(s == 0)
    def _():
        st_ref[...] = jnp.zeros_like(st_ref)
        w_ref[...] = jnp.zeros_like(w_ref)

    @pl.when(chunk_c == 0)
    def _():
        acc_ref[...] = jnp.zeros_like(acc_ref)

    n_groups = h_ref.shape[0] // LANES
    half = h_ref.shape[0] // 2

    def gate_block(slot, prev, aa, g):
        a = chunk_b * rows_per_step + aa
        rows = slice(aa * PEER_N_KEYS, (aa + 1) * PEER_N_KEYS)
        cols = slice(g * LANES, (g + 1) * LANES)
        gate = None
        for hd in range(PEER_HEADS):
            ca = ca_ref[g, hd, pl.ds(a, 1), :]
            e0 = e0_ref[g, hd, pl.ds(a, 1), :]
            term = jnp.where(r1_ref[g, hd] < ca, e1_ref[g, hd] * e0, 0.0)
            gate = term if gate is None else gate + term
        sc = st_ref[prev, rows, cols]
        act = (0.5 * sc) * (1.0 + lax.erf(sc * (2.0 ** -0.5)))
        w_ref[slot, rows, cols] = (gate * act).astype(BF16)

    def step(slot, prev):
        blocks = [(aa, g) for g in range(n_groups) for aa in range(rows_per_step)]
        per = len(blocks) // 4
        for p in range(2):
            tok = slice(p * half, (p + 1) * half)
            st_ref[slot, :, tok] = lax.dot_general(u_ref[...], h_ref[tok, :], NT_DIMS,
                                                   preferred_element_type=F32)
            for aa, g in blocks[(2 * p) * per:(2 * p + 1) * per]:
                gate_block(slot, prev, aa, g)
            acc_ref[:, tok] += jnp.dot(vt_ref[...], w_ref[prev, :, tok],
                                       preferred_element_type=F32)
            for aa, g in blocks[(2 * p + 1) * per:(2 * p + 2) * per]:
                gate_block(slot, prev, aa, g)

    pl.when(s % 2 == 0)(lambda: step(0, 1))
    pl.when(s % 2 == 1)(lambda: step(1, 0))

    @pl.when((chunk_c == n_chunks - 1) & (s >= 2))
    def _():
        x2 = x1_ref[...] + mod_ref[0, 5:6, :] * acc_ref[...].T
        o_ref[...] = _rms(x2) * fn_ref[...]


def _peer_mix(h2, x1, mod, r1, e1, e0, ca, u, vt, fn, ct, ec, seq):
    t, d = h2.shape
    rows = ec // PEER_N_KEYS
    n_chunks = PEER_EXPERTS // ec
    n_steps = (t // ct) * n_chunks

    def item(lag):
        def f(s):
            k = jnp.clip(s - lag, 0, n_steps - 1)
            return k // n_chunks, k % n_chunks
        return f

    a_item, b_item, c_item = item(0), item(1), item(2)
    sel = pl.BlockSpec((ct // LANES, PEER_HEADS, PEER_N_KEYS, LANES), lambda s: (b_item(s)[0], 0, 0, 0))
    tok_c = pl.BlockSpec((ct, d), lambda s: (c_item(s)[0], 0))
    return pl.pallas_call(
        functools.partial(_peer_mix_kernel, rows_per_step=rows, n_chunks=n_chunks, n_steps=n_steps),
        grid=(n_steps + 2,),
        in_specs=[pl.BlockSpec((ct, d), lambda s: (a_item(s)[0], 0)), tok_c,
                  pl.BlockSpec((1, N_MOD, d), lambda s: ((c_item(s)[0] * ct) // seq, 0, 0)),
                  sel, sel, sel, sel,
                  pl.BlockSpec((ec, d), lambda s: (a_item(s)[1], 0)),
                  pl.BlockSpec((d, ec), lambda s: (0, c_item(s)[1])),
                  pl.BlockSpec((1, d), lambda s: (0, 0))],
        out_specs=tok_c,
        out_shape=jax.ShapeDtypeStruct((t, d), F32),
        scratch_shapes=[pltpu.VMEM((d, ct), F32), pltpu.VMEM((2, ec, ct), F32),
                        pltpu.VMEM((2, ec, ct), BF16)],
        compiler_params=_cparams(("arbitrary",)),
        name="peer_mix",
    )(h2, x1, mod, r1, e1, e0, ca, u, vt, fn)


def _rope_tables(seq):
    pos = jnp.arange(seq, dtype=F32)[:, None]

    def cs(dim):
        inv = 1.0 / (ROPE_THETA ** (jnp.arange(0, dim, 2, dtype=F32) / dim))
        ang = pos * inv[None, :]
        return jnp.cos(ang), jnp.sin(ang)

    ca, sa = cs(SWA_HEAD_DIM)
    cosa = jnp.concatenate([ca, ca, ca, ca], axis=1)
    sina = jnp.concatenate([-sa, sa, -sa, sa], axis=1)
    cm, sm = cs(MLA_ROPE_DIM)
    one = jnp.ones((seq, MLA_NOPE_DIM), F32)
    pad1 = jnp.ones((seq, LANES - MLA_QK_DIM), F32)
    cosm = jnp.concatenate([one, cm, cm, pad1], axis=1)
    sinm = jnp.concatenate([0.0 * one, -sm, sm, 0.0 * pad1], axis=1)
    return cosa, sina, cosm, sinm


def _tile(n, pref):
    t = min(n, pref)
    assert n % t == 0, (n, t)
    return t


def kernel(x, c, w_ada, b_ada, w_in, swa_sink, mla_q_norm, w_mla_q_up, mla_kv_norm, w_mla_kv_up,
           out_norm_swa, out_norm_mla, w_out, w_peer_query, peer_sub_keys, peer_expert_u,
           peer_expert_v, final_norm):
    bsz, seq, d = x.shape
    assert w_ada.shape[0] == 1 and seq % SWA_WINDOW == 0 and d == 1024
    qcols = np.concatenate([np.arange(h * SWA_HEAD_DIM, (h + 1) * SWA_HEAD_DIM) for h in SWA_HEAD_ORDER])
    cosa, sina, cosm, sinm = _rope_tables(seq)
    ts = _tile(seq, 512)
    tq_swa = _tile(seq, 256)
    tq_mla = _tile(seq, 256)
    ct_sel = _tile(bsz * seq, 256)
    ct_mix = _tile(seq, 512)

    for l in range(1):
        mod = _ada(c, w_ada[l], b_ada[l]).reshape(bsz, N_MOD, d)

        win = w_in[l]
        pad_cols = jnp.zeros((d, LANES - MLA_ROPE_DIM), win.dtype)
        win_p = jnp.concatenate([win[:, qcols], win[:, SWA_WIDTH:], pad_cols], axis=1).astype(BF16)
        wq_p = jnp.pad(w_mla_q_up[l].reshape(MLA_Q_RANK, MLA_HEADS, MLA_QK_DIM),
                       ((0, 0), (0, 0), (0, LANES - MLA_QK_DIM))).reshape(MLA_Q_RANK, MLA_HEADS * LANES).astype(BF16)
        wkv = w_mla_kv_up[l].reshape(MLA_KV_RANK, MLA_HEADS, MLA_NOPE_DIM + MLA_V_DIM)
        wk_p = jnp.pad(wkv[:, :, :MLA_NOPE_DIM], ((0, 0), (0, 0), (0, LANES - MLA_NOPE_DIM))
                       ).reshape(MLA_KV_RANK, MLA_HEADS * LANES).astype(BF16)
        wv_p = wkv[:, :, MLA_NOPE_DIM:].reshape(MLA_KV_RANK, MLA_WIDTH).astype(BF16)

        qa, ka, va, qm, km, vm = _inproj(
            x, mod, win_p, mla_q_norm[l].reshape(1, -1), wq_p, mla_kv_norm[l].reshape(1, -1), wk_p, wv_p,
            cosa, sina, cosm, sinm, ts)

        oa = _swa(swa_sink[l][np.array(SWA_HEAD_ORDER)], qa, ka, va, tq_swa)
        ob = _mla(qm, km, vm, tq_mla)

        wo = w_out[l]
        wo_p = jnp.concatenate([wo[qcols, :], wo[SWA_WIDTH:, :]], axis=0).astype(BF16)
        x1, h2 = _outproj(oa, ob, x, mod, out_norm_swa[l][qcols].reshape(1, -1),
                          out_norm_mla[l].reshape(1, -1), wo_p, ts)

        h2f = h2.reshape(bsz * seq, d)
        wqt = w_peer_query[l].T.astype(BF16)
        keys = peer_sub_keys[l].reshape(2 * PEER_HEADS, PEER_N_KEYS, PEER_HALF_DIM).astype(BF16)
        r1, e1, e0, ca = _peer_sel(h2f, wqt, keys, ct_sel)

        u = peer_expert_u[l].astype(BF16)
        vt = peer_expert_v[l].T.astype(BF16)
        fn = final_norm.reshape(1, d)
        x = _peer_mix(h2f, x1.reshape(bsz * seq, d), mod, r1, e1, e0, ca, u, vt, fn,
                      ct_mix, 512, seq).reshape(bsz, seq, d)
    return x
```

```python
import functools

import numpy as np
import jax
import jax.numpy as jnp
from jax import lax
from jax.experimental import pallas as pl
from jax.experimental.pallas import tpu as pltpu

F32 = jnp.float32
BF16 = jnp.bfloat16

EPS = 1e-6
ROPE_THETA = 10000.0
NEG_INF = -1e30

SWA_HEADS = 8
SWA_KV_HEADS = 2
SWA_HEAD_DIM = 64
SWA_WINDOW = 128

MLA_HEADS = 8
MLA_Q_RANK = 256
MLA_KV_RANK = 128
MLA_NOPE_DIM = 64
MLA_ROPE_DIM = 32
MLA_V_DIM = 64
MLA_QK_DIM = MLA_NOPE_DIM + MLA_ROPE_DIM

SWA_WIDTH = SWA_HEADS * SWA_HEAD_DIM
MLA_WIDTH = MLA_HEADS * MLA_V_DIM

PEER_HEADS = 8
PEER_N_KEYS = 128
PEER_EXPERTS = PEER_N_KEYS * PEER_N_KEYS
PEER_HALF_DIM = 128
PEER_TOPK = 16

N_MOD = 6
LANES = 128
VMEM_LIMIT = 56 * 1024 * 1024

SWA_HEAD_ORDER = (0, 4, 1, 5, 2, 6, 3, 7)

NT_DIMS = (((1,), (1,)), ((), ()))


def _rms(v):
    return v * lax.rsqrt(jnp.mean(v * v, axis=-1, keepdims=True) + EPS)


def _cparams(sem):
    return pltpu.CompilerParams(dimension_semantics=sem, vmem_limit_bytes=VMEM_LIMIT)


def _ada_kernel(c_ref, w_ref, b_ref, o_ref):
    c = c_ref[...]
    ca = c * jax.nn.sigmoid(c)
    o_ref[...] = jnp.dot(ca, w_ref[...], preferred_element_type=F32,
                         precision=lax.Precision.HIGHEST) + b_ref[...]


def _ada(c, w, b):
    bsz, d = c.shape
    n = w.shape[1]
    tn = 1024
    return pl.pallas_call(
        _ada_kernel,
        grid=(n // tn,),
        in_specs=[pl.BlockSpec((bsz, d), lambda j: (0, 0)),
                  pl.BlockSpec((d, tn), lambda j: (0, j)),
                  pl.BlockSpec((1, tn), lambda j: (0, j))],
        out_specs=pl.BlockSpec((bsz, tn), lambda j: (0, j)),
        out_shape=jax.ShapeDtypeStruct((bsz, n), F32),
        compiler_params=_cparams(("arbitrary",)),
        name="ada",
    )(c, w, b.reshape(1, n))


def _rot_partner(v, half, first_half):
    fwd = pltpu.roll(v, LANES - half, 1)
    bwd = pltpu.roll(v, half, 1)
    return jnp.where(first_half, fwd, bwd)


def _inproj_kernel(x_ref, mod_ref, win_ref, qn_ref, wq_ref, kvn_ref, wk_ref, wv_ref,
                   cosa_ref, sina_ref, cosm_ref, sinm_ref,
                   qa_ref, ka_ref, va_ref, qm_ref, km_ref, vm_ref):
    x = x_ref[0]
    sh = mod_ref[0, 0:1, :]
    sc = mod_ref[0, 1:2, :]
    h = (_rms(x) * (1.0 + sc) + sh).astype(BF16)
    proj = jnp.dot(h, win_ref[...], preferred_element_type=F32)

    lane = lax.broadcasted_iota(jnp.int32, (1, LANES), 1)
    cosa, sina = cosa_ref[...], sina_ref[...]
    first_a = (lane % SWA_HEAD_DIM) < (SWA_HEAD_DIM // 2)

    def rope_a(v):
        return v * cosa + _rot_partner(v, SWA_HEAD_DIM // 2, first_a) * sina

    for j in range(SWA_WIDTH // LANES):
        v = proj[:, j * LANES:(j + 1) * LANES]
        qa_ref[0, :, j * LANES:(j + 1) * LANES] = (rope_a(v) * (SWA_HEAD_DIM ** -0.5)).astype(BF16)
    ka_ref[0] = rope_a(proj[:, 512:640]).astype(BF16)
    va_ref[0] = proj[:, 640:768].astype(BF16)

    cosm, sinm = cosm_ref[...], sinm_ref[...]
    first_m = (lane >= MLA_NOPE_DIM) & (lane < MLA_NOPE_DIM + MLA_ROPE_DIM // 2)

    def rope_m(v):
        return v * cosm + _rot_partner(v, MLA_ROPE_DIM // 2, first_m) * sinm

    ql = (_rms(proj[:, 768:1024]) * qn_ref[...]).astype(BF16)
    qup = jnp.dot(ql, wq_ref[...], preferred_element_type=F32)
    kvl = (_rms(proj[:, 1024:1152]) * kvn_ref[...]).astype(BF16)
    kup = jnp.dot(kvl, wk_ref[...], preferred_element_type=F32)
    vm_ref[0] = jnp.dot(kvl, wv_ref[...], preferred_element_type=F32).astype(BF16)
    kr = rope_m(pltpu.roll(proj[:, 1152:1280], MLA_NOPE_DIM, 1))
    for hd in range(MLA_HEADS):
        sl = slice(hd * LANES, (hd + 1) * LANES)
        qm_ref[0, :, sl] = (rope_m(qup[:, sl]) * (MLA_QK_DIM ** -0.5)).astype(BF16)
        km_ref[0, :, sl] = (kup[:, sl] + kr).astype(BF16)


def _inproj(x, mod, win, qn, wq, kvn, wk, wv, cosa, sina, cosm, sinm, ts):
    bsz, s, d = x.shape
    grid = (s // ts, bsz)
    full = lambda a: pl.BlockSpec(a.shape, lambda i, b: (0,) * a.ndim)
    tab = pl.BlockSpec((ts, LANES), lambda i, b: (i, 0))
    tok = lambda w: pl.BlockSpec((1, ts, w), lambda i, b: (b, i, 0))
    widths = (SWA_WIDTH, LANES, LANES, MLA_HEADS * LANES, MLA_HEADS * LANES, MLA_WIDTH)
    return pl.pallas_call(
        _inproj_kernel,
        grid=grid,
        in_specs=[tok(d), pl.BlockSpec((1, N_MOD, d), lambda i, b: (b, 0, 0)),
                  full(win), full(qn), full(wq), full(kvn), full(wk), full(wv),
                  tab, tab, tab, tab],
        out_specs=[tok(w) for w in widths],
        out_shape=[jax.ShapeDtypeStruct((bsz, s, w), BF16) for w in widths],
        compiler_params=_cparams(("arbitrary", "arbitrary")),
        name="inproj",
    )(x, mod, win, qn, wq, kvn, wk, wv, cosa, sina, cosm, sinm)


def _swa_kernel(sink_ref, q_ref, kp_ref, kc_ref, kn_ref, vp_ref, vc_ref, vn_ref, o_ref, *, tq, seq):
    i = pl.program_id(1)
    nk = tq + 2 * SWA_WINDOW
    k = jnp.concatenate([kp_ref[0], kc_ref[0], kn_ref[0]], axis=0)
    v = jnp.concatenate([vp_ref[0], vc_ref[0], vn_ref[0]], axis=0)
    qpos = i * tq + lax.broadcasted_iota(jnp.int32, (tq, nk), 0)
    kpos = i * tq - SWA_WINDOW + lax.broadcasted_iota(jnp.int32, (tq, nk), 1)
    mask = (jnp.abs(qpos - kpos) <= SWA_WINDOW) & (kpos >= 0) & (kpos < seq)
    lo = lax.broadcasted_iota(jnp.int32, (1, LANES), 1) < SWA_HEAD_DIM
    for j in range(SWA_WIDTH // LANES):
        qv = q_ref[0, :, j * LANES:(j + 1) * LANES]
        halves = []
        for half in range(2):
            qh = jnp.where(lo if half == 0 else jnp.logical_not(lo), qv, jnp.zeros_like(qv))
            s = lax.dot_general(qh, k, NT_DIMS, preferred_element_type=F32)
            s = jnp.where(mask, s, NEG_INF)
            sink = sink_ref[2 * j + half]
            m = jnp.maximum(jnp.max(s, axis=-1, keepdims=True), sink)
            p = jnp.exp(s - m)
            denom = jnp.sum(p, axis=-1, keepdims=True) + jnp.exp(sink - m)
            r = jnp.dot(p.astype(BF16), v, preferred_element_type=F32)
            halves.append(r / denom)
        o_ref[0, :, j * LANES:(j + 1) * LANES] = jnp.where(lo, halves[0], halves[1])


def _swa(sinks, qa, ka, va, tq):
    bsz, s, _ = qa.shape
    nblk = s // SWA_WINDOW
    r = tq // SWA_WINDOW
    prev = pl.BlockSpec((1, SWA_WINDOW, LANES), lambda b, i: (b, jnp.maximum(i * r - 1, 0), 0))
    cur = pl.BlockSpec((1, tq, LANES), lambda b, i: (b, i, 0))
    nxt = pl.BlockSpec((1, SWA_WINDOW, LANES), lambda b, i: (b, jnp.minimum((i + 1) * r, nblk - 1), 0))
    return pl.pallas_call(
        functools.partial(_swa_kernel, tq=tq, seq=s),
        grid=(bsz, s // tq),
        in_specs=[pl.BlockSpec(memory_space=pltpu.SMEM),
                  pl.BlockSpec((1, tq, SWA_WIDTH), lambda b, i: (b, i, 0)),
                  prev, cur, nxt, prev, cur, nxt],
        out_specs=pl.BlockSpec((1, tq, SWA_WIDTH), lambda b, i: (b, i, 0)),
        out_shape=jax.ShapeDtypeStruct((bsz, s, SWA_WIDTH), F32),
        compiler_params=_cparams(("arbitrary", "arbitrary")),
        name="swa",
    )(sinks, qa, ka, ka, ka, va, va, va)


def _mla_kernel(q_ref, k_ref, v_ref, o_ref):
    v = v_ref[0]
    lo = lax.broadcasted_iota(jnp.int32, (1, LANES), 1) < MLA_V_DIM
    halves = []
    for half in range(2):
        sl = slice(half * LANES, (half + 1) * LANES)
        s = lax.dot_general(q_ref[0, :, sl], k_ref[0, :, sl], NT_DIMS, preferred_element_type=F32)
        m = jnp.max(s, axis=-1, keepdims=True)
        p = jnp.exp(s - m)
        denom = jnp.sum(p, axis=-1, keepdims=True)
        r = jnp.dot(p.astype(BF16), v, preferred_element_type=F32)
        halves.append(r / denom)
    o_ref[0] = jnp.where(lo, halves[0], halves[1])


def _mla(qm, km, vm, tq):
    bsz, s, _ = qm.shape
    pairs = MLA_HEADS // 2
    return pl.pallas_call(
        _mla_kernel,
        grid=(bsz, pairs, s // tq),
        in_specs=[pl.BlockSpec((1, tq, 2 * LANES), lambda b, p, i: (b, i, p)),
                  pl.BlockSpec((1, s, 2 * LANES), lambda b, p, i: (b, 0, p)),
                  pl.BlockSpec((1, s, LANES), lambda b, p, i: (b, 0, p))],
        out_specs=pl.BlockSpec((1, tq, LANES), lambda b, p, i: (b, i, p)),
        out_shape=jax.ShapeDtypeStruct((bsz, s, MLA_WIDTH), F32),
        compiler_params=_cparams(("arbitrary", "arbitrary", "arbitrary")),
        name="mla",
    )(qm, km, vm)


def _outproj_kernel(oa_ref, ob_ref, x_ref, mod_ref, ga_ref, gb_ref, wo_ref, x1_ref, h2_ref):
    ma = (_rms(oa_ref[0]) * ga_ref[...]).astype(BF16)
    mb = (_rms(ob_ref[0]) * gb_ref[...]).astype(BF16)
    y = (jnp.dot(ma, wo_ref[0:SWA_WIDTH, :], preferred_element_type=F32)
         + jnp.dot(mb, wo_ref[SWA_WIDTH:, :], preferred_element_type=F32))
    x1 = x_ref[0] + mod_ref[0, 2:3, :] * y
    x1_ref[0] = x1
    h2_ref[0] = (_rms(x1) * (1.0 + mod_ref[0, 4:5, :]) + mod_ref[0, 3:4, :]).astype(BF16)


def _outproj(oa, ob, x, mod, ga, gb, wo, ts):
    bsz, s, d = x.shape
    full = lambda a: pl.BlockSpec(a.shape, lambda i, b: (0,) * a.ndim)
    tok = lambda w: pl.BlockSpec((1, ts, w), lambda i, b: (b, i, 0))
    return pl.pallas_call(
        _outproj_kernel,
        grid=(s // ts, bsz),
        in_specs=[tok(SWA_WIDTH), tok(MLA_WIDTH), tok(d),
                  pl.BlockSpec((1, N_MOD, d), lambda i, b: (b, 0, 0)),
                  full(ga), full(gb), full(wo)],
        out_specs=[tok(d), tok(d)],
        out_shape=[jax.ShapeDtypeStruct((bsz, s, d), F32), jax.ShapeDtypeStruct((bsz, s, d), BF16)],
        compiler_params=_cparams(("arbitrary", "arbitrary")),
        name="outproj",
    )(oa, ob, x, mod, ga, gb, wo)


def _topk_rank(s):
    n = s.shape[0]
    key_id = lax.broadcasted_iota(jnp.int32, s.shape, 0)
    slot = lax.broadcasted_iota(jnp.int32, (PEER_TOPK, s.shape[1]), 0)

    def body(it, carry):
        s, rank, vals = carry
        m = jnp.max(s, axis=0, keepdims=True)
        first = jnp.min(jnp.where(s == m, key_id, n), axis=0, keepdims=True)
        sel = key_id == first
        rank = jnp.where(sel, lax.convert_element_type(it, F32), rank)
        s = jnp.where(sel, -jnp.inf, s)
        vals = jnp.where(slot == it, m, vals)
        return s, rank, vals

    init = (s, jnp.full(s.shape, float(PEER_TOPK), F32), jnp.zeros((PEER_TOPK, s.shape[1]), F32))
    _, rank, vals = lax.fori_loop(0, PEER_TOPK, body, init)
    return vals, rank


def _staircase(v0, v1):
    row = lax.broadcasted_iota(jnp.int32, v0.shape, 0)
    best0 = v0[0:1] + v1[0:1]

    def body(it, carry):
        cnt, front, z = carry
        m = jnp.max(front, axis=0, keepdims=True)
        first = jnp.min(jnp.where(front == m, row, PEER_TOPK), axis=0, keepdims=True)
        sel = row == first
        z = z + jnp.exp(m - best0)
        cnt = cnt + sel.astype(jnp.int32)
        c_sel = jnp.max(jnp.where(sel, cnt, 0), axis=0, keepdims=True)
        nxt = jnp.full_like(m, -jnp.inf)
        for j in range(1, PEER_TOPK):
            nxt = jnp.where(c_sel == j, v1[j:j + 1], nxt)
        front = jnp.where(sel, v0 + nxt, front)
        return cnt, front, z

    init = (jnp.zeros(v0.shape, jnp.int32), v0 + v1[0:1], jnp.zeros_like(best0))
    cnt, _, z = lax.fori_loop(0, PEER_TOPK, body, init)
    return cnt, z


def _batcher_pairs(n):
    pairs = []

    def merge(lo, m, r):
        step = 2 * r
        if step < m:
            merge(lo, m, step)
            merge(lo + r, m, step)
            pairs.extend((i, i + r) for i in range(lo + r, lo + m - r, step))
        else:
            pairs.append((lo, lo + r))

    def sort(lo, m):
        if m > 1:
            sort(lo, m // 2)
            sort(lo + m // 2, m // 2)
            merge(lo, m, 1)

    sort(0, n)
    return pairs


SORT16_PAIRS = _batcher_pairs(PEER_TOPK)
SUBLANES = 8
BLOCKS = PEER_N_KEYS // SUBLANES


def _cmpx(w, i, j):
    w[i], w[j] = jnp.maximum(w[i], w[j]), jnp.minimum(w[i], w[j])


def _sort16(w):
    w = list(w)
    for i, j in SORT16_PAIRS:
        _cmpx(w, i, j)
    return w


def _bitonic16(w):
    w = list(w)
    d = PEER_TOPK // 2
    while d:
        for k in range(PEER_TOPK):
            if not k & d:
                _cmpx(w, k, k + d)
        d //= 2
    return w


def _merge_top16(a, b):
    return _bitonic16([jnp.maximum(a[k], b[PEER_TOPK - 1 - k]) for k in range(PEER_TOPK)])


def _top16_values(blk):
    w = _sort16(blk)
    for shift in (4, 2, 1):
        w = _merge_top16(w, [pltpu.roll(x, shift, 0) for x in w])
    return w


def _ambiguous(blk, v):
    dup = v[0] == v[1]
    for i in range(1, PEER_TOPK - 1):
        dup = dup | (v[i] == v[i + 1])
    n_ge = jnp.where(blk[0] >= v[-1], 1.0, 0.0)
    for k in range(1, BLOCKS):
        n_ge = n_ge + jnp.where(blk[k] >= v[-1], 1.0, 0.0)
    return dup | (jnp.sum(n_ge, axis=0, keepdims=True) != float(PEER_TOPK))


def _candidate_cells():
    return [(i, j) for i in range(PEER_TOPK) for j in range(PEER_TOPK) if (i + 1) * (j + 1) <= PEER_TOPK]


def _select_pairs(v0, v1):
    cells = {ij: v0[ij[0]] + v1[ij[1]] for ij in _candidate_cells()}
    row = lambda i: [cells[i, j] for j in range(PEER_TOPK // (i + 1))]
    l1 = row(0)
    l2 = _bitonic16(row(1) + [cells[i, 0] for i in range(PEER_TOPK - 1, 7, -1)])
    l3 = _sort16(row(2) + row(3) + row(4) + row(5) + row(6))
    t = _merge_top16(_merge_top16(l1, l2), l3)
    d0, d1 = row(7)
    tau = jnp.minimum(t[13], jnp.minimum(jnp.maximum(t[14], d1), jnp.maximum(t[15], d0)))
    best = cells[0, 0]
    inf = jnp.full_like(best, jnp.inf)
    thr = [inf] * PEER_TOPK
    z = jnp.zeros_like(best)
    n_sel = jnp.zeros_like(best)
    for (i, j), c in cells.items():
        sel = c >= tau
        thr[i] = jnp.where(sel, v1[j], thr[i])
        z = z + jnp.where(sel, jnp.exp(c - best), 0.0)
        n_sel = n_sel + jnp.where(sel, 1.0, 0.0)
    return thr, z, n_sel != float(PEER_TOPK)


def _peer_sel_kernel(h_ref, wq_ref, keys_ref, x1_ref, e1_ref, e0_ref, th_ref, s_scr):
    ct = h_ref.shape[0]
    qt = lax.dot_general(wq_ref[...], h_ref[...], NT_DIMS, preferred_element_type=F32).astype(BF16)
    for hp in range(2 * PEER_HEADS):
        res = jnp.dot(keys_ref[hp], qt[hp * PEER_HALF_DIM:(hp + 1) * PEER_HALF_DIM, :],
                      preferred_element_type=F32)
        for g in range(ct // LANES):
            s_scr[g, hp] = res[:, g * LANES:(g + 1) * LANES]

    for g in range(ct // LANES):

        def per_head(hd, carry):
            vreg = lambda p, k: s_scr[g, 2 * hd + p, k * SUBLANES:(k + 1) * SUBLANES, :]
            b0 = [vreg(0, k) for k in range(BLOCKS)]
            b1 = [vreg(1, k) for k in range(BLOCKS)]
            v0 = _top16_values(b0)
            v1 = _top16_values(b1)
            thr, z, bad_pairs = _select_pairs(v0, v1)
            bad = _ambiguous(b0, v0) | _ambiguous(b1, v1) | bad_pairs
            inv_z = 1.0 / z
            for k in range(BLOCKS):
                rows = slice(k * SUBLANES, (k + 1) * SUBLANES)
                th = jnp.full_like(b0[k], jnp.inf)
                for i in range(PEER_TOPK):
                    th = jnp.where(b0[k] == v0[i], thr[i], th)
                x1_ref[g, hd, rows, :] = b1[k]
                e1_ref[g, hd, rows, :] = jnp.exp(b1[k] - v1[0])
                e0_ref[g, hd, rows, :] = jnp.exp(b0[k] - v0[0]) * inv_z
                th_ref[g, hd, rows, :] = th

            @pl.when(jnp.max(jnp.where(bad, 1.0, 0.0)) > 0.0)
            def _():
                s0 = s_scr[g, 2 * hd]
                s1 = s_scr[g, 2 * hd + 1]
                u0, rank0 = _topk_rank(s0)
                u1, rank1 = _topk_rank(s1)
                cnt, zz = _staircase(u0, u1)
                cnt_f = cnt.astype(F32)
                ca = jnp.zeros_like(s0)
                for i in range(PEER_TOPK):
                    ca = jnp.where(rank0 == float(i), cnt_f[i:i + 1], ca)
                x1_ref[g, hd] = -rank1
                e1_ref[g, hd] = jnp.exp(s1 - u1[0:1])
                e0_ref[g, hd] = jnp.exp(s0 - u0[0:1]) / zz
                th_ref[g, hd] = 1.0 - ca

            return carry

        lax.fori_loop(0, PEER_HEADS, per_head, 0)


def _peer_sel(h2, wqt, keys, ct):
    t, d = h2.shape
    full = lambda a: pl.BlockSpec(a.shape, lambda i: (0,) * a.ndim)
    out = pl.BlockSpec((ct // LANES, PEER_HEADS, PEER_N_KEYS, LANES), lambda i: (i, 0, 0, 0))
    shp = jax.ShapeDtypeStruct((t // LANES, PEER_HEADS, PEER_N_KEYS, LANES), F32)
    return pl.pallas_call(
        _peer_sel_kernel,
        grid=(t // ct,),
        in_specs=[pl.BlockSpec((ct, d), lambda i: (i, 0)), full(wqt), full(keys)],
        out_specs=[out, out, out, out],
        out_shape=[shp, shp, shp, shp],
        scratch_shapes=[pltpu.VMEM((ct // LANES, 2 * PEER_HEADS, PEER_N_KEYS, LANES), F32)],
        compiler_params=_cparams(("arbitrary",)),
        name="peer_sel",
    )(h2, wqt, keys)


def _peer_mix_kernel(h_ref, x1_ref, mod_ref, k1_ref, e1_ref, e0_ref, th_ref, u_ref, vt_ref, fn_ref,
                     o_ref, acc_ref, st_ref, w_ref, *, rows_per_step, n_chunks, n_steps):
    s = pl.program_id(0)
    slot = s % 2
    prev = 1 - slot
    chunk_b = jnp.clip(s - 1, 0, n_steps - 1) % n_chunks
    chunk_c = jnp.clip(s - 2, 0, n_steps - 1) % n_chunks

    @pl.when(s == 0)
    def _():
        st_ref[...] = jnp.zeros_like(st_ref)
        w_ref[...] = jnp.zeros_like(w_ref)

    @pl.when(chunk_c == 0)
    def _():
        acc_ref[...] = jnp.zeros_like(acc_ref)

    n_groups = h_ref.shape[0] // LANES
    half = h_ref.shape[0] // 2

    def gate_block(slot, prev, aa, g):
        a = chunk_b * rows_per_step + aa
        rows = slice(aa * PEER_N_KEYS, (aa + 1) * PEER_N_KEYS)
        cols = slice(g * LANES, (g + 1) * LANES)
        gate = None
        for hd in range(PEER_HEADS):
            th = th_ref[g, hd, pl.ds(a, 1), :]
            e0 = e0_ref[g, hd, pl.ds(a, 1), :]
            term = jnp.where(k1_ref[g, hd] >= th, e1_ref[g, hd] * e0, 0.0)
            gate = term if gate is None else gate + term
        sc = st_ref[prev, rows, cols]
        act = (0.5 * sc) * (1.0 + lax.erf(sc * (2.0 ** -0.5)))
        w_ref[slot, rows, cols] = (gate * act).astype(BF16)

    def step(slot, prev):
        blocks = [(aa, g) for g in range(n_groups) for aa in range(rows_per_step)]
        per = len(blocks) // 4
        for p in range(2):
            tok = slice(p * half, (p + 1) * half)
            st_ref[slot, :, tok] = lax.dot_general(u_ref[...], h_ref[tok, :], NT_DIMS,
                                                   preferred_element_type=F32)
            for aa, g in blocks[(2 * p) * per:(2 * p + 1) * per]:
                gate_block(slot, prev, aa, g)
            acc_ref[:, tok] += jnp.dot(vt_ref[...], w_ref[prev, :, tok],
                                       preferred_element_type=F32)
            for aa, g in blocks[(2 * p + 1) * per:(2 * p + 2) * per]:
                gate_block(slot, prev, aa, g)

    pl.when(s % 2 == 0)(lambda: step(0, 1))
    pl.when(s % 2 == 1)(lambda: step(1, 0))

    @pl.when((chunk_c == n_chunks - 1) & (s >= 2))
    def _():
        x2 = x1_ref[...] + mod_ref[0, 5:6, :] * acc_ref[...].T
        o_ref[...] = _rms(x2) * fn_ref[...]


def _peer_mix(h2, x1, mod, r1, e1, e0, ca, u, vt, fn, ct, ec, seq):
    t, d = h2.shape
    rows = ec // PEER_N_KEYS
    n_chunks = PEER_EXPERTS // ec
    n_steps = (t // ct) * n_chunks

    def item(lag):
        def f(s):
            k = jnp.clip(s - lag, 0, n_steps - 1)
            return k // n_chunks, k % n_chunks
        return f

    a_item, b_item, c_item = item(0), item(1), item(2)
    sel = pl.BlockSpec((ct // LANES, PEER_HEADS, PEER_N_KEYS, LANES), lambda s: (b_item(s)[0], 0, 0, 0))
    tok_c = pl.BlockSpec((ct, d), lambda s: (c_item(s)[0], 0))
    return pl.pallas_call(
        functools.partial(_peer_mix_kernel, rows_per_step=rows, n_chunks=n_chunks, n_steps=n_steps),
        grid=(n_steps + 2,),
        in_specs=[pl.BlockSpec((ct, d), lambda s: (a_item(s)[0], 0)), tok_c,
                  pl.BlockSpec((1, N_MOD, d), lambda s: ((c_item(s)[0] * ct) // seq, 0, 0)),
                  sel, sel, sel, sel,
                  pl.BlockSpec((ec, d), lambda s: (a_item(s)[1], 0)),
                  pl.BlockSpec((d, ec), lambda s: (0, c_item(s)[1])),
                  pl.BlockSpec((1, d), lambda s: (0, 0))],
        out_specs=tok_c,
        out_shape=jax.ShapeDtypeStruct((t, d), F32),
        scratch_shapes=[pltpu.VMEM((d, ct), F32), pltpu.VMEM((2, ec, ct), F32),
                        pltpu.VMEM((2, ec, ct), BF16)],
        compiler_params=_cparams(("arbitrary",)),
        name="peer_mix",
    )(h2, x1, mod, r1, e1, e0, ca, u, vt, fn)


def _rope_tables(seq):
    pos = jnp.arange(seq, dtype=F32)[:, None]

    def cs(dim):
        inv = 1.0 / (ROPE_THETA ** (jnp.arange(0, dim, 2, dtype=F32) / dim))
        ang = pos * inv[None, :]
        return jnp.cos(ang), jnp.sin(ang)

    ca, sa = cs(SWA_HEAD_DIM)
    cosa = jnp.concatenate([ca, ca, ca, ca], axis=1)
    sina = jnp.concatenate([-sa, sa, -sa, sa], axis=1)
    cm, sm = cs(MLA_ROPE_DIM)
    one = jnp.ones((seq, MLA_NOPE_DIM), F32)
    pad1 = jnp.ones((seq, LANES - MLA_QK_DIM), F32)
    cosm = jnp.concatenate([one, cm, cm, pad1], axis=1)
    sinm = jnp.concatenate([0.0 * one, -sm, sm, 0.0 * pad1], axis=1)
    return cosa, sina, cosm, sinm


def _tile(n, pref):
    t = min(n, pref)
    assert n % t == 0, (n, t)
    return t


def kernel(x, c, w_ada, b_ada, w_in, swa_sink, mla_q_norm, w_mla_q_up, mla_kv_norm, w_mla_kv_up,
           out_norm_swa, out_norm_mla, w_out, w_peer_query, peer_sub_keys, peer_expert_u,
           peer_expert_v, final_norm):
    bsz, seq, d = x.shape
    assert w_ada.shape[0] == 1 and seq % SWA_WINDOW == 0 and d == 1024
    qcols = np.concatenate([np.arange(h * SWA_HEAD_DIM, (h + 1) * SWA_HEAD_DIM) for h in SWA_HEAD_ORDER])
    cosa, sina, cosm, sinm = _rope_tables(seq)
    ts = _tile(seq, 512)
    tq_swa = _tile(seq, 256)
    tq_mla = _tile(seq, 256)
    ct_sel = _tile(bsz * seq, 256)
    ct_mix = _tile(seq, 512)

    for l in range(1):
        mod = _ada(c, w_ada[l], b_ada[l]).reshape(bsz, N_MOD, d)

        win = w_in[l]
        pad_cols = jnp.zeros((d, LANES - MLA_ROPE_DIM), win.dtype)
        win_p = jnp.concatenate([win[:, qcols], win[:, SWA_WIDTH:], pad_cols], axis=1).astype(BF16)
        wq_p = jnp.pad(w_mla_q_up[l].reshape(MLA_Q_RANK, MLA_HEADS, MLA_QK_DIM),
                       ((0, 0), (0, 0), (0, LANES - MLA_QK_DIM))).reshape(MLA_Q_RANK, MLA_HEADS * LANES).astype(BF16)
        wkv = w_mla_kv_up[l].reshape(MLA_KV_RANK, MLA_HEADS, MLA_NOPE_DIM + MLA_V_DIM)
        wk_p = jnp.pad(wkv[:, :, :MLA_NOPE_DIM], ((0, 0), (0, 0), (0, LANES - MLA_NOPE_DIM))
                       ).reshape(MLA_KV_RANK, MLA_HEADS * LANES).astype(BF16)
        wv_p = wkv[:, :, MLA_NOPE_DIM:].reshape(MLA_KV_RANK, MLA_WIDTH).astype(BF16)

        qa, ka, va, qm, km, vm = _inproj(
            x, mod, win_p, mla_q_norm[l].reshape(1, -1), wq_p, mla_kv_norm[l].reshape(1, -1), wk_p, wv_p,
            cosa, sina, cosm, sinm, ts)

        oa = _swa(swa_sink[l][np.array(SWA_HEAD_ORDER)], qa, ka, va, tq_swa)
        ob = _mla(qm, km, vm, tq_mla)

        wo = w_out[l]
        wo_p = jnp.concatenate([wo[qcols, :], wo[SWA_WIDTH:, :]], axis=0).astype(BF16)
        x1, h2 = _outproj(oa, ob, x, mod, out_norm_swa[l][qcols].reshape(1, -1),
                          out_norm_mla[l].reshape(1, -1), wo_p, ts)

        h2f = h2.reshape(bsz * seq, d)
        wqt = w_peer_query[l].T.astype(BF16)
        keys = peer_sub_keys[l].reshape(2 * PEER_HEADS, PEER_N_KEYS, PEER_HALF_DIM).astype(BF16)
        r1, e1, e0, ca = _peer_sel(h2f, wqt, keys, ct_sel)

        u = peer_expert_u[l].astype(BF16)
        vt = peer_expert_v[l].T.astype(BF16)
        fn = final_norm.reshape(1, d)
        x = _peer_mix(h2f, x1.reshape(bsz * seq, d), mod, r1, e1, e0, ca, u, vt, fn,
                      ct_mix, 512, seq).reshape(bsz, seq, d)
    return x
```

```python
import functools

import numpy as np
import jax
import jax.numpy as jnp
from jax import lax
from jax.experimental import pallas as pl
from jax.experimental.pallas import tpu as pltpu

F32 = jnp.float32
BF16 = jnp.bfloat16

EPS = 1e-6
ROPE_THETA = 10000.0
NEG_INF = -1e30

SWA_HEADS = 8
SWA_KV_HEADS = 2
SWA_HEAD_DIM = 64
SWA_WINDOW = 128

MLA_HEADS = 8
MLA_Q_RANK = 256
MLA_KV_RANK = 128
MLA_NOPE_DIM = 64
MLA_ROPE_DIM = 32
MLA_V_DIM = 64
MLA_QK_DIM = MLA_NOPE_DIM + MLA_ROPE_DIM

SWA_WIDTH = SWA_HEADS * SWA_HEAD_DIM
MLA_WIDTH = MLA_HEADS * MLA_V_DIM

PEER_HEADS = 8
PEER_N_KEYS = 128
PEER_EXPERTS = PEER_N_KEYS * PEER_N_KEYS
PEER_HALF_DIM = 128
PEER_TOPK = 16

N_MOD = 6
LANES = 128
VMEM_LIMIT = 56 * 1024 * 1024

SWA_HEAD_ORDER = (0, 4, 1, 5, 2, 6, 3, 7)

NT_DIMS = (((1,), (1,)), ((), ()))


def _rms(v):
    return v * lax.rsqrt(jnp.mean(v * v, axis=-1, keepdims=True) + EPS)


def _cparams(sem):
    return pltpu.CompilerParams(dimension_semantics=sem, vmem_limit_bytes=VMEM_LIMIT)


def _ada_kernel(c_ref, w_ref, b_ref, o_ref):
    c = c_ref[...]
    ca = c * jax.nn.sigmoid(c)
    o_ref[...] = jnp.dot(ca, w_ref[...], preferred_element_type=F32,
                         precision=lax.Precision.HIGHEST) + b_ref[...]


def _ada(c, w, b):
    bsz, d = c.shape
    n = w.shape[1]
    tn = 1024
    return pl.pallas_call(
        _ada_kernel,
        grid=(n // tn,),
        in_specs=[pl.BlockSpec((bsz, d), lambda j: (0, 0)),
                  pl.BlockSpec((d, tn), lambda j: (0, j)),
                  pl.BlockSpec((1, tn), lambda j: (0, j))],
        out_specs=pl.BlockSpec((bsz, tn), lambda j: (0, j)),
        out_shape=jax.ShapeDtypeStruct((bsz, n), F32),
        compiler_params=_cparams(("arbitrary",)),
        name="ada",
    )(c, w, b.reshape(1, n))


def _rot_partner(v, half, first_half):
    fwd = pltpu.roll(v, LANES - half, 1)
    bwd = pltpu.roll(v, half, 1)
    return jnp.where(first_half, fwd, bwd)


def _inproj_kernel(x_ref, mod_ref, win_ref, qn_ref, wq_ref, kvn_ref, wk_ref, wv_ref,
                   cosa_ref, sina_ref, cosm_ref, sinm_ref,
                   qa_ref, ka_ref, va_ref, qm_ref, km_ref, vm_ref):
    x = x_ref[0]
    sh = mod_ref[0, 0:1, :]
    sc = mod_ref[0, 1:2, :]
    h = (_rms(x) * (1.0 + sc) + sh).astype(BF16)
    proj = jnp.dot(h, win_ref[...], preferred_element_type=F32)

    lane = lax.broadcasted_iota(jnp.int32, (1, LANES), 1)
    cosa, sina = cosa_ref[...], sina_ref[...]
    first_a = (lane % SWA_HEAD_DIM) < (SWA_HEAD_DIM // 2)

    def rope_a(v):
        return v * cosa + _rot_partner(v, SWA_HEAD_DIM // 2, first_a) * sina

    for j in range(SWA_WIDTH // LANES):
        v = proj[:, j * LANES:(j + 1) * LANES]
        qa_ref[0, :, j * LANES:(j + 1) * LANES] = (rope_a(v) * (SWA_HEAD_DIM ** -0.5)).astype(BF16)
    ka_ref[0] = rope_a(proj[:, 512:640]).astype(BF16)
    va_ref[0] = proj[:, 640:768].astype(BF16)

    cosm, sinm = cosm_ref[...], sinm_ref[...]
    first_m = (lane >= MLA_NOPE_DIM) & (lane < MLA_NOPE_DIM + MLA_ROPE_DIM // 2)

    def rope_m(v):
        return v * cosm + _rot_partner(v, MLA_ROPE_DIM // 2, first_m) * sinm

    ql = (_rms(proj[:, 768:1024]) * qn_ref[...]).astype(BF16)
    qup = jnp.dot(ql, wq_ref[...], preferred_element_type=F32)
    kvl = (_rms(proj[:, 1024:1152]) * kvn_ref[...]).astype(BF16)
    kup = jnp.dot(kvl, wk_ref[...], preferred_element_type=F32)
    vm_ref[0] = jnp.dot(kvl, wv_ref[...], preferred_element_type=F32).astype(BF16)
    kr = rope_m(pltpu.roll(proj[:, 1152:1280], MLA_NOPE_DIM, 1))
    for hd in range(MLA_HEADS):
        sl = slice(hd * LANES, (hd + 1) * LANES)
        qm_ref[0, :, sl] = (rope_m(qup[:, sl]) * (MLA_QK_DIM ** -0.5)).astype(BF16)
        km_ref[0, :, sl] = (kup[:, sl] + kr).astype(BF16)


def _inproj(x, mod, win, qn, wq, kvn, wk, wv, cosa, sina, cosm, sinm, ts):
    bsz, s, d = x.shape
    grid = (s // ts, bsz)
    full = lambda a: pl.BlockSpec(a.shape, lambda i, b: (0,) * a.ndim)
    tab = pl.BlockSpec((ts, LANES), lambda i, b: (i, 0))
    tok = lambda w: pl.BlockSpec((1, ts, w), lambda i, b: (b, i, 0))
    widths = (SWA_WIDTH, LANES, LANES, MLA_HEADS * LANES, MLA_HEADS * LANES, MLA_WIDTH)
    return pl.pallas_call(
        _inproj_kernel,
        grid=grid,
        in_specs=[tok(d), pl.BlockSpec((1, N_MOD, d), lambda i, b: (b, 0, 0)),
                  full(win), full(qn), full(wq), full(kvn), full(wk), full(wv),
                  tab, tab, tab, tab],
        out_specs=[tok(w) for w in widths],
        out_shape=[jax.ShapeDtypeStruct((bsz, s, w), BF16) for w in widths],
        compiler_params=_cparams(("arbitrary", "arbitrary")),
        name="inproj",
    )(x, mod, win, qn, wq, kvn, wk, wv, cosa, sina, cosm, sinm)


def _swa_kernel(sink_ref, q_ref, kp_ref, kc_ref, kn_ref, vp_ref, vc_ref, vn_ref, o_ref, *, tq, seq):
    i = pl.program_id(1)
    nk = tq + 2 * SWA_WINDOW
    k = jnp.concatenate([kp_ref[0], kc_ref[0], kn_ref[0]], axis=0)
    v = jnp.concatenate([vp_ref[0], vc_ref[0], vn_ref[0]], axis=0)
    qpos = i * tq + lax.broadcasted_iota(jnp.int32, (tq, nk), 0)
    kpos = i * tq - SWA_WINDOW + lax.broadcasted_iota(jnp.int32, (tq, nk), 1)
    mask = (jnp.abs(qpos - kpos) <= SWA_WINDOW) & (kpos >= 0) & (kpos < seq)
    lo = lax.broadcasted_iota(jnp.int32, (1, LANES), 1) < SWA_HEAD_DIM
    for j in range(SWA_WIDTH // LANES):
        qv = q_ref[0, :, j * LANES:(j + 1) * LANES]
        halves = []
        for half in range(2):
            qh = jnp.where(lo if half == 0 else jnp.logical_not(lo), qv, jnp.zeros_like(qv))
            s = lax.dot_general(qh, k, NT_DIMS, preferred_element_type=F32)
            s = jnp.where(mask, s, NEG_INF)
            sink = sink_ref[2 * j + half]
            m = jnp.maximum(jnp.max(s, axis=-1, keepdims=True), sink)
            p = jnp.exp(s - m)
            denom = jnp.sum(p, axis=-1, keepdims=True) + jnp.exp(sink - m)
            r = jnp.dot(p.astype(BF16), v, preferred_element_type=F32)
            halves.append(r / denom)
        o_ref[0, :, j * LANES:(j + 1) * LANES] = jnp.where(lo, halves[0], halves[1])


def _swa(sinks, qa, ka, va, tq):
    bsz, s, _ = qa.shape
    nblk = s // SWA_WINDOW
    r = tq // SWA_WINDOW
    prev = pl.BlockSpec((1, SWA_WINDOW, LANES), lambda b, i: (b, jnp.maximum(i * r - 1, 0), 0))
    cur = pl.BlockSpec((1, tq, LANES), lambda b, i: (b, i, 0))
    nxt = pl.BlockSpec((1, SWA_WINDOW, LANES), lambda b, i: (b, jnp.minimum((i + 1) * r, nblk - 1), 0))
    return pl.pallas_call(
        functools.partial(_swa_kernel, tq=tq, seq=s),
        grid=(bsz, s // tq),
        in_specs=[pl.BlockSpec(memory_space=pltpu.SMEM),
                  pl.BlockSpec((1, tq, SWA_WIDTH), lambda b, i: (b, i, 0)),
                  prev, cur, nxt, prev, cur, nxt],
        out_specs=pl.BlockSpec((1, tq, SWA_WIDTH), lambda b, i: (b, i, 0)),
        out_shape=jax.ShapeDtypeStruct((bsz, s, SWA_WIDTH), F32),
        compiler_params=_cparams(("arbitrary", "arbitrary")),
        name="swa",
    )(sinks, qa, ka, ka, ka, va, va, va)


def _mla_kernel(q_ref, k_ref, v_ref, o_ref):
    v = v_ref[0]
    lo = lax.broadcasted_iota(jnp.int32, (1, LANES), 1) < MLA_V_DIM
    halves = []
    for half in range(2):
        sl = slice(half * LANES, (half + 1) * LANES)
        s = lax.dot_general(q_ref[0, :, sl], k_ref[0, :, sl], NT_DIMS, preferred_element_type=F32)
        m = jnp.max(s, axis=-1, keepdims=True)
        p = jnp.exp(s - m)
        denom = jnp.sum(p, axis=-1, keepdims=True)
        r = jnp.dot(p.astype(BF16), v, preferred_element_type=F32)
        halves.append(r / denom)
    o_ref[0] = jnp.where(lo, halves[0], halves[1])


def _mla(qm, km, vm, tq):
    bsz, s, _ = qm.shape
    pairs = MLA_HEADS // 2
    return pl.pallas_call(
        _mla_kernel,
        grid=(bsz, pairs, s // tq),
        in_specs=[pl.BlockSpec((1, tq, 2 * LANES), lambda b, p, i: (b, i, p)),
                  pl.BlockSpec((1, s, 2 * LANES), lambda b, p, i: (b, 0, p)),
                  pl.BlockSpec((1, s, LANES), lambda b, p, i: (b, 0, p))],
        out_specs=pl.BlockSpec((1, tq, LANES), lambda b, p, i: (b, i, p)),
        out_shape=jax.ShapeDtypeStruct((bsz, s, MLA_WIDTH), F32),
        compiler_params=_cparams(("arbitrary", "arbitrary", "arbitrary")),
        name="mla",
    )(qm, km, vm)


def _outproj_kernel(oa_ref, ob_ref, x_ref, mod_ref, ga_ref, gb_ref, wo_ref, x1_ref, h2t_ref):
    ma = (_rms(oa_ref[0]) * ga_ref[...]).astype(BF16)
    mb = (_rms(ob_ref[0]) * gb_ref[...]).astype(BF16)
    y = (jnp.dot(ma, wo_ref[0:SWA_WIDTH, :], preferred_element_type=F32)
         + jnp.dot(mb, wo_ref[SWA_WIDTH:, :], preferred_element_type=F32))
    x1 = x_ref[0] + mod_ref[0, 2:3, :] * y
    x1_ref[0] = x1
    h2 = _rms(x1) * (1.0 + mod_ref[0, 4:5, :]) + mod_ref[0, 3:4, :]
    h2t_ref[...] = h2.T.astype(BF16)


def _outproj(oa, ob, x, mod, ga, gb, wo, ts):
    bsz, s, d = x.shape
    nt = s // ts
    full = lambda a: pl.BlockSpec(a.shape, lambda i, b: (0,) * a.ndim)
    tok = lambda w: pl.BlockSpec((1, ts, w), lambda i, b: (b, i, 0))
    return pl.pallas_call(
        _outproj_kernel,
        grid=(nt, bsz),
        in_specs=[tok(SWA_WIDTH), tok(MLA_WIDTH), tok(d),
                  pl.BlockSpec((1, N_MOD, d), lambda i, b: (b, 0, 0)),
                  full(ga), full(gb), full(wo)],
        out_specs=[tok(d), pl.BlockSpec((d, ts), lambda i, b: (0, b * nt + i))],
        out_shape=[jax.ShapeDtypeStruct((bsz, s, d), F32), jax.ShapeDtypeStruct((d, bsz * s), BF16)],
        compiler_params=_cparams(("arbitrary", "arbitrary")),
        name="outproj",
    )(oa, ob, x, mod, ga, gb, wo)


def _topk_rank(s):
    n = s.shape[0]
    key_id = lax.broadcasted_iota(jnp.int32, s.shape, 0)
    slot = lax.broadcasted_iota(jnp.int32, (PEER_TOPK, s.shape[1]), 0)

    def body(it, carry):
        s, rank, vals = carry
        m = jnp.max(s, axis=0, keepdims=True)
        first = jnp.min(jnp.where(s == m, key_id, n), axis=0, keepdims=True)
        sel = key_id == first
        rank = jnp.where(sel, lax.convert_element_type(it, F32), rank)
        s = jnp.where(sel, -jnp.inf, s)
        vals = jnp.where(slot == it, m, vals)
        return s, rank, vals

    init = (s, jnp.full(s.shape, float(PEER_TOPK), F32), jnp.zeros((PEER_TOPK, s.shape[1]), F32))
    _, rank, vals = lax.fori_loop(0, PEER_TOPK, body, init)
    return vals, rank


def _staircase(v0, v1):
    row = lax.broadcasted_iota(jnp.int32, v0.shape, 0)
    best0 = v0[0:1] + v1[0:1]

    def body(it, carry):
        cnt, front, z = carry
        m = jnp.max(front, axis=0, keepdims=True)
        first = jnp.min(jnp.where(front == m, row, PEER_TOPK), axis=0, keepdims=True)
        sel = row == first
        z = z + jnp.exp(m - best0)
        cnt = cnt + sel.astype(jnp.int32)
        c_sel = jnp.max(jnp.where(sel, cnt, 0), axis=0, keepdims=True)
        nxt = jnp.full_like(m, -jnp.inf)
        for j in range(1, PEER_TOPK):
            nxt = jnp.where(c_sel == j, v1[j:j + 1], nxt)
        front = jnp.where(sel, v0 + nxt, front)
        return cnt, front, z

    init = (jnp.zeros(v0.shape, jnp.int32), v0 + v1[0:1], jnp.zeros_like(best0))
    cnt, _, z = lax.fori_loop(0, PEER_TOPK, body, init)
    return cnt, z


def _batcher_pairs(n):
    pairs = []

    def merge(lo, m, r):
        step = 2 * r
        if step < m:
            merge(lo, m, step)
            merge(lo + r, m, step)
            pairs.extend((i, i + r) for i in range(lo + r, lo + m - r, step))
        else:
            pairs.append((lo, lo + r))

    def sort(lo, m):
        if m > 1:
            sort(lo, m // 2)
            sort(lo + m // 2, m // 2)
            merge(lo, m, 1)

    sort(0, n)
    return pairs


SORT16_PAIRS = _batcher_pairs(PEER_TOPK)
SUBLANES = 8
BLOCKS = PEER_N_KEYS // SUBLANES


def _cmpx(w, i, j):
    w[i], w[j] = jnp.maximum(w[i], w[j]), jnp.minimum(w[i], w[j])


def _sort16(w):
    w = list(w)
    for i, j in SORT16_PAIRS:
        _cmpx(w, i, j)
    return w


def _bitonic16(w):
    w = list(w)
    d = PEER_TOPK // 2
    while d:
        for k in range(PEER_TOPK):
            if not k & d:
                _cmpx(w, k, k + d)
        d //= 2
    return w


def _merge_top16(a, b):
    return _bitonic16([jnp.maximum(a[k], b[PEER_TOPK - 1 - k]) for k in range(PEER_TOPK)])


def _top16_values(blk):
    w = _sort16(blk)
    for shift in (4, 2, 1):
        w = _merge_top16(w, [pltpu.roll(x, shift, 0) for x in w])
    return w


def _ambiguous(blk, v):
    dup = v[0] == v[1]
    for i in range(1, PEER_TOPK - 1):
        dup = dup | (v[i] == v[i + 1])
    n_ge = jnp.where(blk[0] >= v[-1], 1.0, 0.0)
    for k in range(1, BLOCKS):
        n_ge = n_ge + jnp.where(blk[k] >= v[-1], 1.0, 0.0)
    return dup | (jnp.sum(n_ge, axis=0, keepdims=True) != float(PEER_TOPK))


def _candidate_cells():
    return [(i, j) for i in range(PEER_TOPK) for j in range(PEER_TOPK) if (i + 1) * (j + 1) <= PEER_TOPK]


def _select_pairs(v0, v1):
    cells = {ij: v0[ij[0]] + v1[ij[1]] for ij in _candidate_cells()}
    row = lambda i: [cells[i, j] for j in range(PEER_TOPK // (i + 1))]
    l1 = row(0)
    l2 = _bitonic16(row(1) + [cells[i, 0] for i in range(PEER_TOPK - 1, 7, -1)])
    l3 = _sort16(row(2) + row(3) + row(4) + row(5) + row(6))
    t = _merge_top16(_merge_top16(l1, l2), l3)
    d0, d1 = row(7)
    tau = jnp.minimum(t[13], jnp.minimum(jnp.maximum(t[14], d1), jnp.maximum(t[15], d0)))
    best = cells[0, 0]
    inf = jnp.full_like(best, jnp.inf)
    thr = [inf] * PEER_TOPK
    z = jnp.zeros_like(best)
    n_sel = jnp.zeros_like(best)
    for (i, j), c in cells.items():
        sel = c >= tau
        thr[i] = jnp.where(sel, v1[j], thr[i])
        z = z + jnp.where(sel, jnp.exp(c - best), 0.0)
        n_sel = n_sel + jnp.where(sel, 1.0, 0.0)
    return thr, z, n_sel != float(PEER_TOPK)


def _peer_sel_kernel(ht_ref, wq_ref, keys_ref, x1_ref, e1_ref, e0_ref, th_ref, s_scr):
    ct = ht_ref.shape[1]
    qt = jnp.dot(wq_ref[...], ht_ref[...], preferred_element_type=F32).astype(BF16)
    for hp in range(2 * PEER_HEADS):
        res = jnp.dot(keys_ref[hp], qt[hp * PEER_HALF_DIM:(hp + 1) * PEER_HALF_DIM, :],
                      preferred_element_type=F32)
        for g in range(ct // LANES):
            s_scr[g, hp] = res[:, g * LANES:(g + 1) * LANES]

    for g in range(ct // LANES):

        def per_head(hd, carry):
            vreg = lambda p, k: s_scr[g, 2 * hd + p, k * SUBLANES:(k + 1) * SUBLANES, :]
            b0 = [vreg(0, k) for k in range(BLOCKS)]
            b1 = [vreg(1, k) for k in range(BLOCKS)]
            v0 = _top16_values(b0)
            v1 = _top16_values(b1)
            thr, z, bad_pairs = _select_pairs(v0, v1)
            bad = _ambiguous(b0, v0) | _ambiguous(b1, v1) | bad_pairs
            inv_z = 1.0 / z
            for k in range(BLOCKS):
                rows = slice(k * SUBLANES, (k + 1) * SUBLANES)
                th = jnp.full_like(b0[k], jnp.inf)
                for i in range(PEER_TOPK):
                    th = jnp.where(b0[k] == v0[i], thr[i], th)
                x1_ref[g, hd, rows, :] = b1[k]
                e1_ref[g, hd, rows, :] = jnp.exp(b1[k] - v1[0])
                e0_ref[g, hd, rows, :] = jnp.exp(b0[k] - v0[0]) * inv_z
                th_ref[g, hd, rows, :] = th

            @pl.when(jnp.max(jnp.where(bad, 1.0, 0.0)) > 0.0)
            def _():
                s0 = s_scr[g, 2 * hd]
                s1 = s_scr[g, 2 * hd + 1]
                u0, rank0 = _topk_rank(s0)
                u1, rank1 = _topk_rank(s1)
                cnt, zz = _staircase(u0, u1)
                cnt_f = cnt.astype(F32)
                ca = jnp.zeros_like(s0)
                for i in range(PEER_TOPK):
                    ca = jnp.where(rank0 == float(i), cnt_f[i:i + 1], ca)
                x1_ref[g, hd] = -rank1
                e1_ref[g, hd] = jnp.exp(s1 - u1[0:1])
                e0_ref[g, hd] = jnp.exp(s0 - u0[0:1]) / zz
                th_ref[g, hd] = 1.0 - ca

            return carry

        lax.fori_loop(0, PEER_HEADS, per_head, 0)


def _peer_sel(h2t, wqt, keys, ct):
    d, t = h2t.shape
    full = lambda a: pl.BlockSpec(a.shape, lambda i: (0,) * a.ndim)
    out = pl.BlockSpec((ct // LANES, PEER_HEADS, PEER_N_KEYS, LANES), lambda i: (i, 0, 0, 0))
    shp = jax.ShapeDtypeStruct((t // LANES, PEER_HEADS, PEER_N_KEYS, LANES), F32)
    return pl.pallas_call(
        _peer_sel_kernel,
        grid=(t // ct,),
        in_specs=[pl.BlockSpec((d, ct), lambda i: (0, i)), full(wqt), full(keys)],
        out_specs=[out, out, out, out],
        out_shape=[shp, shp, shp, shp],
        scratch_shapes=[pltpu.VMEM((ct // LANES, 2 * PEER_HEADS, PEER_N_KEYS, LANES), F32)],
        compiler_params=_cparams(("arbitrary",)),
        name="peer_sel",
    )(h2t, wqt, keys)


def _peer_mix_kernel(ht_ref, x1_ref, mod_ref, k1_ref, e1_ref, e0_ref, th_ref, u_ref, vt_ref, fn_ref,
                     o_ref, acc_ref, st_a, st_b, w_a, w_b, *, rows_per_step, n_chunks, n_steps):
    s = pl.program_id(0)
    slot = s % 2
    prev = 1 - slot
    chunk_b = jnp.clip(s - 1, 0, n_steps - 1) % n_chunks
    chunk_c = jnp.clip(s - 2, 0, n_steps - 1) % n_chunks

    @pl.when(s == 0)
    def _():
        for ref in (st_a, st_b, w_a, w_b):
            ref[...] = jnp.zeros_like(ref)

    @pl.when(chunk_c == 0)
    def _():
        acc_ref[...] = jnp.zeros_like(acc_ref)

    n_groups = ht_ref.shape[1] // LANES
    half = ht_ref.shape[1] // 2
    st_refs = (st_a, st_b)
    w_refs = (w_a, w_b)

    def gate_block(slot, prev, aa, g):
        a = chunk_b * rows_per_step + aa
        rows = slice(aa * PEER_N_KEYS, (aa + 1) * PEER_N_KEYS)
        cols = slice(g * LANES, (g + 1) * LANES)
        gate = None
        for hd in range(PEER_HEADS):
            th = th_ref[g, hd, pl.ds(a, 1), :]
            e0 = e0_ref[g, hd, pl.ds(a, 1), :]
            term = jnp.where(k1_ref[g, hd] >= th, e1_ref[g, hd] * e0, 0.0)
            gate = term if gate is None else gate + term
        sc = st_refs[prev][rows, cols]
        act = (0.5 * sc) * (1.0 + lax.erf(sc * (2.0 ** -0.5)))
        w_refs[slot][rows, cols] = (gate * act).astype(BF16)

    def step(slot, prev):
        def score_piece(p):
            tok = slice(p * half, (p + 1) * half)
            st_refs[slot][:, tok] = jnp.dot(u_ref[...], ht_ref[:, tok],
                                            preferred_element_type=F32)

        def value_piece(q):
            dr = slice(q * dblk, (q + 1) * dblk)
            acc_ref[dr, :] += jnp.dot(vt_ref[dr, :], w_refs[prev][...],
                                      preferred_element_type=F32)

        dblk = acc_ref.shape[0] // 4
        pieces = [(score_piece, 0), (value_piece, 0), (value_piece, 1),
                  (score_piece, 1), (value_piece, 2), (value_piece, 3)]
        blocks = [(aa, g) for g in range(n_groups) for aa in range(rows_per_step)]
        starts = [(k * len(blocks)) // len(pieces) for k in range(len(pieces))]
        for b, (aa, g) in enumerate(blocks):
            for (fn, arg), at in zip(pieces, starts):
                if at == b:
                    fn(arg)
            gate_block(slot, prev, aa, g)

    pl.when(s % 2 == 0)(lambda: step(0, 1))
    pl.when(s % 2 == 1)(lambda: step(1, 0))

    @pl.when((chunk_c == n_chunks - 1) & (s >= 2))
    def _():
        x2 = x1_ref[...] + mod_ref[0, 5:6, :] * acc_ref[...].T
        o_ref[...] = _rms(x2) * fn_ref[...]


def _peer_mix(h2t, x1, mod, k1, e1, e0, th, u, vt, fn, ct, ec, seq):
    d, t = h2t.shape
    rows = ec // PEER_N_KEYS
    n_chunks = PEER_EXPERTS // ec
    n_steps = (t // ct) * n_chunks

    def item(lag):
        def f(s):
            k = jnp.clip(s - lag, 0, n_steps - 1)
            return k // n_chunks, k % n_chunks
        return f

    a_item, b_item, c_item = item(0), item(1), item(2)
    sel = pl.BlockSpec((ct // LANES, PEER_HEADS, PEER_N_KEYS, LANES), lambda s: (b_item(s)[0], 0, 0, 0))
    tok_c = pl.BlockSpec((ct, d), lambda s: (c_item(s)[0], 0))
    return pl.pallas_call(
        functools.partial(_peer_mix_kernel, rows_per_step=rows, n_chunks=n_chunks, n_steps=n_steps),
        grid=(n_steps + 2,),
        in_specs=[pl.BlockSpec((d, ct), lambda s: (0, a_item(s)[0])), tok_c,
                  pl.BlockSpec((1, N_MOD, d), lambda s: ((c_item(s)[0] * ct) // seq, 0, 0)),
                  sel, sel, sel, sel,
                  pl.BlockSpec((ec, d), lambda s: (a_item(s)[1], 0)),
                  pl.BlockSpec((d, ec), lambda s: (0, c_item(s)[1])),
                  pl.BlockSpec((1, d), lambda s: (0, 0))],
        out_specs=tok_c,
        out_shape=jax.ShapeDtypeStruct((t, d), F32),
        scratch_shapes=[pltpu.VMEM((d, ct), F32), pltpu.VMEM((ec, ct), F32), pltpu.VMEM((ec, ct), F32),
                        pltpu.VMEM((ec, ct), BF16), pltpu.VMEM((ec, ct), BF16)],
        compiler_params=_cparams(("arbitrary",)),
        name="peer_mix",
    )(h2t, x1, mod, k1, e1, e0, th, u, vt, fn)


def _rope_tables(seq):
    pos = jnp.arange(seq, dtype=F32)[:, None]

    def cs(dim):
        inv = 1.0 / (ROPE_THETA ** (jnp.arange(0, dim, 2, dtype=F32) / dim))
        ang = pos * inv[None, :]
        return jnp.cos(ang), jnp.sin(ang)

    ca, sa = cs(SWA_HEAD_DIM)
    cosa = jnp.concatenate([ca, ca, ca, ca], axis=1)
    sina = jnp.concatenate([-sa, sa, -sa, sa], axis=1)
    cm, sm = cs(MLA_ROPE_DIM)
    one = jnp.ones((seq, MLA_NOPE_DIM), F32)
    pad1 = jnp.ones((seq, LANES - MLA_QK_DIM), F32)
    cosm = jnp.concatenate([one, cm, cm, pad1], axis=1)
    sinm = jnp.concatenate([0.0 * one, -sm, sm, 0.0 * pad1], axis=1)
    return cosa, sina, cosm, sinm


def _tile(n, pref):
    t = min(n, pref)
    assert n % t == 0, (n, t)
    return t


def kernel(x, c, w_ada, b_ada, w_in, swa_sink, mla_q_norm, w_mla_q_up, mla_kv_norm, w_mla_kv_up,
           out_norm_swa, out_norm_mla, w_out, w_peer_query, peer_sub_keys, peer_expert_u,
           peer_expert_v, final_norm):
    bsz, seq, d = x.shape
    assert w_ada.shape[0] == 1 and seq % SWA_WINDOW == 0 and d == 1024
    qcols = np.concatenate([np.arange(h * SWA_HEAD_DIM, (h + 1) * SWA_HEAD_DIM) for h in SWA_HEAD_ORDER])
    cosa, sina, cosm, sinm = _rope_tables(seq)
    ts = _tile(seq, 512)
    tq_swa = _tile(seq, 256)
    tq_mla = _tile(seq, 256)
    ct_sel = _tile(bsz * seq, 256)
    ct_mix = _tile(seq, 512)

    for l in range(1):
        mod = _ada(c, w_ada[l], b_ada[l]).reshape(bsz, N_MOD, d)

        win = w_in[l]
        pad_cols = jnp.zeros((d, LANES - MLA_ROPE_DIM), win.dtype)
        win_p = jnp.concatenate([win[:, qcols], win[:, SWA_WIDTH:], pad_cols], axis=1).astype(BF16)
        wq_p = jnp.pad(w_mla_q_up[l].reshape(MLA_Q_RANK, MLA_HEADS, MLA_QK_DIM),
                       ((0, 0), (0, 0), (0, LANES - MLA_QK_DIM))).reshape(MLA_Q_RANK, MLA_HEADS * LANES).astype(BF16)
        wkv = w_mla_kv_up[l].reshape(MLA_KV_RANK, MLA_HEADS, MLA_NOPE_DIM + MLA_V_DIM)
        wk_p = jnp.pad(wkv[:, :, :MLA_NOPE_DIM], ((0, 0), (0, 0), (0, LANES - MLA_NOPE_DIM))
                       ).reshape(MLA_KV_RANK, MLA_HEADS * LANES).astype(BF16)
        wv_p = wkv[:, :, MLA_NOPE_DIM:].reshape(MLA_KV_RANK, MLA_WIDTH).astype(BF16)

        qa, ka, va, qm, km, vm = _inproj(
            x, mod, win_p, mla_q_norm[l].reshape(1, -1), wq_p, mla_kv_norm[l].reshape(1, -1), wk_p, wv_p,
            cosa, sina, cosm, sinm, ts)

        oa = _swa(swa_sink[l][np.array(SWA_HEAD_ORDER)], qa, ka, va, tq_swa)
        ob = _mla(qm, km, vm, tq_mla)

        wo = w_out[l]
        wo_p = jnp.concatenate([wo[qcols, :], wo[SWA_WIDTH:, :]], axis=0).astype(BF16)
        x1, h2t = _outproj(oa, ob, x, mod, out_norm_swa[l][qcols].reshape(1, -1),
                           out_norm_mla[l].reshape(1, -1), wo_p, ts)

        wqt = w_peer_query[l].T.astype(BF16)
        keys = peer_sub_keys[l].reshape(2 * PEER_HEADS, PEER_N_KEYS, PEER_HALF_DIM).astype(BF16)
        k1, e1, e0, th = _peer_sel(h2t, wqt, keys, ct_sel)

        u = peer_expert_u[l].astype(BF16)
        vt = peer_expert_v[l].T.astype(BF16)
        fn = final_norm.reshape(1, d)
        x = _peer_mix(h2t, x1.reshape(bsz * seq, d), mod, k1, e1, e0, th, u, vt, fn,
                      ct_mix, 512, seq).reshape(bsz, seq, d)
    return x
```

```python
import functools

import numpy as np
import jax
import jax.numpy as jnp
from jax import lax
from jax.experimental import pallas as pl
from jax.experimental.pallas import tpu as pltpu

F32 = jnp.float32
BF16 = jnp.bfloat16

EPS = 1e-6
ROPE_THETA = 10000.0
NEG_INF = -1e30

SWA_HEADS = 8
SWA_KV_HEADS = 2
SWA_HEAD_DIM = 64
SWA_WINDOW = 128

MLA_HEADS = 8
MLA_Q_RANK = 256
MLA_KV_RANK = 128
MLA_NOPE_DIM = 64
MLA_ROPE_DIM = 32
MLA_V_DIM = 64
MLA_QK_DIM = MLA_NOPE_DIM + MLA_ROPE_DIM

SWA_WIDTH = SWA_HEADS * SWA_HEAD_DIM
MLA_WIDTH = MLA_HEADS * MLA_V_DIM

PEER_HEADS = 8
PEER_N_KEYS = 128
PEER_EXPERTS = PEER_N_KEYS * PEER_N_KEYS
PEER_HALF_DIM = 128
PEER_TOPK = 16

N_MOD = 6
LANES = 128
VMEM_LIMIT = 56 * 1024 * 1024

SWA_HEAD_ORDER = (0, 4, 1, 5, 2, 6, 3, 7)

NT_DIMS = (((1,), (1,)), ((), ()))


def _rms(v):
    return v * lax.rsqrt(jnp.mean(v * v, axis=-1, keepdims=True) + EPS)


def _cparams(sem):
    return pltpu.CompilerParams(dimension_semantics=sem, vmem_limit_bytes=VMEM_LIMIT)


def _ada_kernel(c_ref, w_ref, b_ref, o_ref):
    c = c_ref[...]
    ca = c * jax.nn.sigmoid(c)
    o_ref[...] = jnp.dot(ca, w_ref[...], preferred_element_type=F32,
                         precision=lax.Precision.HIGHEST) + b_ref[...]


def _ada(c, w, b):
    bsz, d = c.shape
    n = w.shape[1]
    tn = 1024
    return pl.pallas_call(
        _ada_kernel,
        grid=(n // tn,),
        in_specs=[pl.BlockSpec((bsz, d), lambda j: (0, 0)),
                  pl.BlockSpec((d, tn), lambda j: (0, j)),
                  pl.BlockSpec((1, tn), lambda j: (0, j))],
        out_specs=pl.BlockSpec((bsz, tn), lambda j: (0, j)),
        out_shape=jax.ShapeDtypeStruct((bsz, n), F32),
        compiler_params=_cparams(("arbitrary",)),
        name="ada",
    )(c, w, b.reshape(1, n))


def _rot_partner(v, half, first_half):
    fwd = pltpu.roll(v, LANES - half, 1)
    bwd = pltpu.roll(v, half, 1)
    return jnp.where(first_half, fwd, bwd)


def _inproj_kernel(x_ref, mod_ref, win_ref, qn_ref, wq_ref, kvn_ref, wk_ref, wv_ref,
                   cosa_ref, sina_ref, cosm_ref, sinm_ref,
                   qa_ref, ka_ref, va_ref, qm_ref, km_ref, vm_ref):
    x = x_ref[0]
    sh = mod_ref[0, 0:1, :]
    sc = mod_ref[0, 1:2, :]
    h = (_rms(x) * (1.0 + sc) + sh).astype(BF16)
    proj = jnp.dot(h, win_ref[...], preferred_element_type=F32)

    lane = lax.broadcasted_iota(jnp.int32, (1, LANES), 1)
    cosa, sina = cosa_ref[...], sina_ref[...]
    first_a = (lane % SWA_HEAD_DIM) < (SWA_HEAD_DIM // 2)

    def rope_a(v):
        return v * cosa + _rot_partner(v, SWA_HEAD_DIM // 2, first_a) * sina

    for j in range(SWA_WIDTH // LANES):
        v = proj[:, j * LANES:(j + 1) * LANES]
        qa_ref[0, :, j * LANES:(j + 1) * LANES] = (rope_a(v) * (SWA_HEAD_DIM ** -0.5)).astype(BF16)
    ka_ref[0] = rope_a(proj[:, 512:640]).astype(BF16)
    va_ref[0] = proj[:, 640:768].astype(BF16)

    cosm, sinm = cosm_ref[...], sinm_ref[...]
    first_m = (lane >= MLA_NOPE_DIM) & (lane < MLA_NOPE_DIM + MLA_ROPE_DIM // 2)

    def rope_m(v):
        return v * cosm + _rot_partner(v, MLA_ROPE_DIM // 2, first_m) * sinm

    ql = (_rms(proj[:, 768:1024]) * qn_ref[...]).astype(BF16)
    qup = jnp.dot(ql, wq_ref[...], preferred_element_type=F32)
    kvl = (_rms(proj[:, 1024:1152]) * kvn_ref[...]).astype(BF16)
    kup = jnp.dot(kvl, wk_ref[...], preferred_element_type=F32)
    vm_ref[0] = jnp.dot(kvl, wv_ref[...], preferred_element_type=F32).astype(BF16)
    kr = rope_m(pltpu.roll(proj[:, 1152:1280], MLA_NOPE_DIM, 1))
    for hd in range(MLA_HEADS):
        sl = slice(hd * LANES, (hd + 1) * LANES)
        qm_ref[0, :, sl] = (rope_m(qup[:, sl]) * (MLA_QK_DIM ** -0.5)).astype(BF16)
        km_ref[0, :, sl] = (kup[:, sl] + kr).astype(BF16)


def _inproj(x, mod, win, qn, wq, kvn, wk, wv, cosa, sina, cosm, sinm, ts):
    bsz, s, d = x.shape
    grid = (s // ts, bsz)
    full = lambda a: pl.BlockSpec(a.shape, lambda i, b: (0,) * a.ndim)
    tab = pl.BlockSpec((ts, LANES), lambda i, b: (i, 0))
    tok = lambda w: pl.BlockSpec((1, ts, w), lambda i, b: (b, i, 0))
    widths = (SWA_WIDTH, LANES, LANES, MLA_HEADS * LANES, MLA_HEADS * LANES, MLA_WIDTH)
    return pl.pallas_call(
        _inproj_kernel,
        grid=grid,
        in_specs=[tok(d), pl.BlockSpec((1, N_MOD, d), lambda i, b: (b, 0, 0)),
                  full(win), full(qn), full(wq), full(kvn), full(wk), full(wv),
                  tab, tab, tab, tab],
        out_specs=[tok(w) for w in widths],
        out_shape=[jax.ShapeDtypeStruct((bsz, s, w), BF16) for w in widths],
        compiler_params=_cparams(("arbitrary", "arbitrary")),
        name="inproj",
    )(x, mod, win, qn, wq, kvn, wk, wv, cosa, sina, cosm, sinm)


def _swa_kernel(sink_ref, q_ref, kp_ref, kc_ref, kn_ref, vp_ref, vc_ref, vn_ref, o_ref, *, tq, seq):
    i = pl.program_id(1)
    nk = tq + 2 * SWA_WINDOW
    k = jnp.concatenate([kp_ref[0], kc_ref[0], kn_ref[0]], axis=0)
    v = jnp.concatenate([vp_ref[0], vc_ref[0], vn_ref[0]], axis=0)
    qpos = i * tq + lax.broadcasted_iota(jnp.int32, (tq, nk), 0)
    kpos = i * tq - SWA_WINDOW + lax.broadcasted_iota(jnp.int32, (tq, nk), 1)
    mask = (jnp.abs(qpos - kpos) <= SWA_WINDOW) & (kpos >= 0) & (kpos < seq)
    lo = lax.broadcasted_iota(jnp.int32, (1, LANES), 1) < SWA_HEAD_DIM
    for j in range(SWA_WIDTH // LANES):
        qv = q_ref[0, :, j * LANES:(j + 1) * LANES]
        halves = []
        for half in range(2):
            qh = jnp.where(lo if half == 0 else jnp.logical_not(lo), qv, jnp.zeros_like(qv))
            s = lax.dot_general(qh, k, NT_DIMS, preferred_element_type=F32)
            s = jnp.where(mask, s, NEG_INF)
            sink = sink_ref[2 * j + half]
            m = jnp.maximum(jnp.max(s, axis=-1, keepdims=True), sink)
            p = jnp.exp(s - m)
            denom = jnp.sum(p, axis=-1, keepdims=True) + jnp.exp(sink - m)
            r = jnp.dot(p.astype(BF16), v, preferred_element_type=F32)
            halves.append(r / denom)
        o_ref[0, :, j * LANES:(j + 1) * LANES] = jnp.where(lo, halves[0], halves[1])


def _swa(sinks, qa, ka, va, tq):
    bsz, s, _ = qa.shape
    nblk = s // SWA_WINDOW
    r = tq // SWA_WINDOW
    prev = pl.BlockSpec((1, SWA_WINDOW, LANES), lambda b, i: (b, jnp.maximum(i * r - 1, 0), 0))
    cur = pl.BlockSpec((1, tq, LANES), lambda b, i: (b, i, 0))
    nxt = pl.BlockSpec((1, SWA_WINDOW, LANES), lambda b, i: (b, jnp.minimum((i + 1) * r, nblk - 1), 0))
    return pl.pallas_call(
        functools.partial(_swa_kernel, tq=tq, seq=s),
        grid=(bsz, s // tq),
        in_specs=[pl.BlockSpec(memory_space=pltpu.SMEM),
                  pl.BlockSpec((1, tq, SWA_WIDTH), lambda b, i: (b, i, 0)),
                  prev, cur, nxt, prev, cur, nxt],
        out_specs=pl.BlockSpec((1, tq, SWA_WIDTH), lambda b, i: (b, i, 0)),
        out_shape=jax.ShapeDtypeStruct((bsz, s, SWA_WIDTH), F32),
        compiler_params=_cparams(("arbitrary", "arbitrary")),
        name="swa",
    )(sinks, qa, ka, ka, ka, va, va, va)


def _mla_kernel(q_ref, k_ref, v_ref, o_ref):
    v = v_ref[0]
    lo = lax.broadcasted_iota(jnp.int32, (1, LANES), 1) < MLA_V_DIM
    halves = []
    for half in range(2):
        sl = slice(half * LANES, (half + 1) * LANES)
        s = lax.dot_general(q_ref[0, :, sl], k_ref[0, :, sl], NT_DIMS, preferred_element_type=F32)
        m = jnp.max(s, axis=-1, keepdims=True)
        p = jnp.exp(s - m)
        denom = jnp.sum(p, axis=-1, keepdims=True)
        r = jnp.dot(p.astype(BF16), v, preferred_element_type=F32)
        halves.append(r / denom)
    o_ref[0] = jnp.where(lo, halves[0], halves[1])


def _mla(qm, km, vm, tq):
    bsz, s, _ = qm.shape
    pairs = MLA_HEADS // 2
    return pl.pallas_call(
        _mla_kernel,
        grid=(bsz, pairs, s // tq),
        in_specs=[pl.BlockSpec((1, tq, 2 * LANES), lambda b, p, i: (b, i, p)),
                  pl.BlockSpec((1, s, 2 * LANES), lambda b, p, i: (b, 0, p)),
                  pl.BlockSpec((1, s, LANES), lambda b, p, i: (b, 0, p))],
        out_specs=pl.BlockSpec((1, tq, LANES), lambda b, p, i: (b, i, p)),
        out_shape=jax.ShapeDtypeStruct((bsz, s, MLA_WIDTH), F32),
        compiler_params=_cparams(("arbitrary", "arbitrary", "arbitrary")),
        name="mla",
    )(qm, km, vm)


def _outproj_kernel(oa_ref, ob_ref, x_ref, mod_ref, ga_ref, gb_ref, wo_ref, x1_ref, h2t_ref):
    ma = (_rms(oa_ref[0]) * ga_ref[...]).astype(BF16)
    mb = (_rms(ob_ref[0]) * gb_ref[...]).astype(BF16)
    y = (jnp.dot(ma, wo_ref[0:SWA_WIDTH, :], preferred_element_type=F32)
         + jnp.dot(mb, wo_ref[SWA_WIDTH:, :], preferred_element_type=F32))
    x1 = x_ref[0] + mod_ref[0, 2:3, :] * y
    x1_ref[0] = x1
    h2 = _rms(x1) * (1.0 + mod_ref[0, 4:5, :]) + mod_ref[0, 3:4, :]
    h2t_ref[...] = h2.T.astype(BF16)


def _outproj(oa, ob, x, mod, ga, gb, wo, ts):
    bsz, s, d = x.shape
    nt = s // ts
    full = lambda a: pl.BlockSpec(a.shape, lambda i, b: (0,) * a.ndim)
    tok = lambda w: pl.BlockSpec((1, ts, w), lambda i, b: (b, i, 0))
    return pl.pallas_call(
        _outproj_kernel,
        grid=(nt, bsz),
        in_specs=[tok(SWA_WIDTH), tok(MLA_WIDTH), tok(d),
                  pl.BlockSpec((1, N_MOD, d), lambda i, b: (b, 0, 0)),
                  full(ga), full(gb), full(wo)],
        out_specs=[tok(d), pl.BlockSpec((d, ts), lambda i, b: (0, b * nt + i))],
        out_shape=[jax.ShapeDtypeStruct((bsz, s, d), F32), jax.ShapeDtypeStruct((d, bsz * s), BF16)],
        compiler_params=_cparams(("arbitrary", "arbitrary")),
        name="outproj",
    )(oa, ob, x, mod, ga, gb, wo)


def _topk_rank(s):
    n = s.shape[0]
    key_id = lax.broadcasted_iota(jnp.int32, s.shape, 0)
    slot = lax.broadcasted_iota(jnp.int32, (PEER_TOPK, s.shape[1]), 0)

    def body(it, carry):
        s, rank, vals = carry
        m = jnp.max(s, axis=0, keepdims=True)
        first = jnp.min(jnp.where(s == m, key_id, n), axis=0, keepdims=True)
        sel = key_id == first
        rank = jnp.where(sel, lax.convert_element_type(it, F32), rank)
        s = jnp.where(sel, -jnp.inf, s)
        vals = jnp.where(slot == it, m, vals)
        return s, rank, vals

    init = (s, jnp.full(s.shape, float(PEER_TOPK), F32), jnp.zeros((PEER_TOPK, s.shape[1]), F32))
    _, rank, vals = lax.fori_loop(0, PEER_TOPK, body, init)
    return vals, rank


def _staircase(v0, v1):
    row = lax.broadcasted_iota(jnp.int32, v0.shape, 0)
    best0 = v0[0:1] + v1[0:1]

    def body(it, carry):
        cnt, front, z = carry
        m = jnp.max(front, axis=0, keepdims=True)
        first = jnp.min(jnp.where(front == m, row, PEER_TOPK), axis=0, keepdims=True)
        sel = row == first
        z = z + jnp.exp(m - best0)
        cnt = cnt + sel.astype(jnp.int32)
        c_sel = jnp.max(jnp.where(sel, cnt, 0), axis=0, keepdims=True)
        nxt = jnp.full_like(m, -jnp.inf)
        for j in range(1, PEER_TOPK):
            nxt = jnp.where(c_sel == j, v1[j:j + 1], nxt)
        front = jnp.where(sel, v0 + nxt, front)
        return cnt, front, z

    init = (jnp.zeros(v0.shape, jnp.int32), v0 + v1[0:1], jnp.zeros_like(best0))
    cnt, _, z = lax.fori_loop(0, PEER_TOPK, body, init)
    return cnt, z


def _batcher_pairs(n):
    pairs = []

    def merge(lo, m, r):
        step = 2 * r
        if step < m:
            merge(lo, m, step)
            merge(lo + r, m, step)
            pairs.extend((i, i + r) for i in range(lo + r, lo + m - r, step))
        else:
            pairs.append((lo, lo + r))

    def sort(lo, m):
        if m > 1:
            sort(lo, m // 2)
            sort(lo + m // 2, m // 2)
            merge(lo, m, 1)

    sort(0, n)
    return pairs


SORT16_PAIRS = _batcher_pairs(PEER_TOPK)
SUBLANES = 8
BLOCKS = PEER_N_KEYS // SUBLANES


def _cmpx(w, i, j):
    w[i], w[j] = jnp.maximum(w[i], w[j]), jnp.minimum(w[i], w[j])


def _sort16(w):
    w = list(w)
    for i, j in SORT16_PAIRS:
        _cmpx(w, i, j)
    return w


def _bitonic16(w):
    w = list(w)
    d = PEER_TOPK // 2
    while d:
        for k in range(PEER_TOPK):
            if not k & d:
                _cmpx(w, k, k + d)
        d //= 2
    return w


def _merge_top16(a, b):
    return _bitonic16([jnp.maximum(a[k], b[PEER_TOPK - 1 - k]) for k in range(PEER_TOPK)])


def _top16_values(blk):
    w = _sort16(blk)
    for shift in (4, 2, 1):
        w = _merge_top16(w, [pltpu.roll(x, shift, 0) for x in w])
    return w


def _ambiguous(blk, v):
    dup = v[0] == v[1]
    for i in range(1, PEER_TOPK - 1):
        dup = dup | (v[i] == v[i + 1])
    n_ge = jnp.where(blk[0] >= v[-1], 1.0, 0.0)
    for k in range(1, BLOCKS):
        n_ge = n_ge + jnp.where(blk[k] >= v[-1], 1.0, 0.0)
    return dup | (jnp.sum(n_ge, axis=0, keepdims=True) != float(PEER_TOPK))


def _candidate_cells():
    return [(i, j) for i in range(PEER_TOPK) for j in range(PEER_TOPK) if (i + 1) * (j + 1) <= PEER_TOPK]


def _select_pairs(v0, v1):
    cells = {ij: v0[ij[0]] + v1[ij[1]] for ij in _candidate_cells()}
    row = lambda i: [cells[i, j] for j in range(PEER_TOPK // (i + 1))]
    l1 = row(0)
    l2 = _bitonic16(row(1) + [cells[i, 0] for i in range(PEER_TOPK - 1, 7, -1)])
    l3 = _sort16(row(2) + row(3) + row(4) + row(5) + row(6))
    t = _merge_top16(_merge_top16(l1, l2), l3)
    d0, d1 = row(7)
    tau = jnp.minimum(t[13], jnp.minimum(jnp.maximum(t[14], d1), jnp.maximum(t[15], d0)))
    best = cells[0, 0]
    cnt = [jnp.zeros_like(best)] * PEER_TOPK
    z = jnp.zeros_like(best)
    for (i, j), c in cells.items():
        sel = c >= tau
        cnt[i] = cnt[i] + jnp.where(sel, 1.0, 0.0)
        z = z + jnp.where(sel, jnp.exp(c - best), 0.0)
    n_sel = cnt[0]
    for i in range(1, PEER_TOPK):
        n_sel = n_sel + cnt[i]
    return cnt, z, n_sel != float(PEER_TOPK)


def _pack_pairs(v):
    return pltpu.bitcast(v.astype(BF16), jnp.uint32)


def _peer_sel_kernel(ht_ref, wq_ref, keys_ref, x1_ref, e1_ref, e0_ref, th_ref, s_scr):
    ct = ht_ref.shape[1]
    qt = jnp.dot(wq_ref[...], ht_ref[...], preferred_element_type=F32).astype(BF16)
    for hp in range(2 * PEER_HEADS):
        res = jnp.dot(keys_ref[hp], qt[hp * PEER_HALF_DIM:(hp + 1) * PEER_HALF_DIM, :],
                      preferred_element_type=F32)
        for g in range(ct // LANES):
            s_scr[g, hp] = res[:, g * LANES:(g + 1) * LANES]

    for g in range(ct // LANES):

        def per_head(hd, carry):
            vreg = lambda p, k: s_scr[g, 2 * hd + p, k * SUBLANES:(k + 1) * SUBLANES, :]
            b0 = [vreg(0, k) for k in range(BLOCKS)]
            b1 = [vreg(1, k) for k in range(BLOCKS)]
            v0 = _top16_values(b0)
            v1 = _top16_values(b1)
            cnt, z, bad_pairs = _select_pairs(v0, v1)
            bad = _ambiguous(b0, v0) | _ambiguous(b1, v1) | bad_pairs
            inv_z = 1.0 / z
            th_rows = [1.0 - c for c in cnt]
            x1_blk, e1_blk = [], []
            for k in range(BLOCKS):
                rows = slice(k * SUBLANES, (k + 1) * SUBLANES)
                th = jnp.ones_like(b0[k])
                x1 = jnp.full_like(b1[k], -float(PEER_TOPK))
                for i in range(PEER_TOPK):
                    th = jnp.where(b0[k] == v0[i], th_rows[i], th)
                    x1 = jnp.where(b1[k] == v1[i], -float(i), x1)
                x1_blk.append(x1)
                e1_blk.append(jnp.exp(b1[k] - v1[0]))
                e0_ref[g, hd, rows, :] = jnp.exp(b0[k] - v0[0]) * inv_z
                th_ref[g, hd, rows, :] = th
            x1_ref[g, hd] = _pack_pairs(jnp.concatenate(x1_blk, axis=0))
            e1_ref[g, hd] = _pack_pairs(jnp.concatenate(e1_blk, axis=0))

            @pl.when(jnp.max(jnp.where(bad, 1.0, 0.0)) > 0.0)
            def _():
                s0 = s_scr[g, 2 * hd]
                s1 = s_scr[g, 2 * hd + 1]
                u0, rank0 = _topk_rank(s0)
                u1, rank1 = _topk_rank(s1)
                cnt, zz = _staircase(u0, u1)
                cnt_f = cnt.astype(F32)
                ca = jnp.zeros_like(s0)
                for i in range(PEER_TOPK):
                    ca = jnp.where(rank0 == float(i), cnt_f[i:i + 1], ca)
                x1_ref[g, hd] = _pack_pairs(-rank1)
                e1_ref[g, hd] = _pack_pairs(jnp.exp(s1 - u1[0:1]))
                e0_ref[g, hd] = jnp.exp(s0 - u0[0:1]) / zz
                th_ref[g, hd] = 1.0 - ca

            return carry

        lax.fori_loop(0, PEER_HEADS, per_head, 0)


def _peer_sel(h2t, wqt, keys, ct):
    d, t = h2t.shape
    full = lambda a: pl.BlockSpec(a.shape, lambda i: (0,) * a.ndim)
    out = pl.BlockSpec((ct // LANES, PEER_HEADS, PEER_N_KEYS, LANES), lambda i: (i, 0, 0, 0))
    shp = jax.ShapeDtypeStruct((t // LANES, PEER_HEADS, PEER_N_KEYS, LANES), F32)
    out_pk = pl.BlockSpec((ct // LANES, PEER_HEADS, PEER_N_KEYS // 2, LANES), lambda i: (i, 0, 0, 0))
    shp_pk = jax.ShapeDtypeStruct((t // LANES, PEER_HEADS, PEER_N_KEYS // 2, LANES), jnp.uint32)
    return pl.pallas_call(
        _peer_sel_kernel,
        grid=(t // ct,),
        in_specs=[pl.BlockSpec((d, ct), lambda i: (0, i)), full(wqt), full(keys)],
        out_specs=[out_pk, out_pk, out, out],
        out_shape=[shp_pk, shp_pk, shp, shp],
        scratch_shapes=[pltpu.VMEM((ct // LANES, 2 * PEER_HEADS, PEER_N_KEYS, LANES), F32)],
        compiler_params=_cparams(("arbitrary",)),
        name="peer_sel",
    )(h2t, wqt, keys)


def _peer_mix_kernel(ht_ref, x1_ref, mod_ref, k1_ref, e1_ref, e0_ref, th_ref, u_ref, vt_ref, fn_ref,
                     o_ref, acc_ref, st_a, st_b, w_a, w_b, pk_ref, *, rows_per_step, n_chunks, n_steps):
    s = pl.program_id(0)
    slot = s % 2
    prev = 1 - slot
    chunk_b = jnp.clip(s - 1, 0, n_steps - 1) % n_chunks
    chunk_c = jnp.clip(s - 2, 0, n_steps - 1) % n_chunks

    @pl.when(s == 0)
    def _():
        for ref in (st_a, st_b, w_a, w_b):
            ref[...] = jnp.zeros_like(ref)

    @pl.when(chunk_c == 0)
    def _():
        acc_ref[...] = jnp.zeros_like(acc_ref)

    n_groups = ht_ref.shape[1] // LANES
    half = ht_ref.shape[1] // 2
    st_refs = (st_a, st_b)
    w_refs = (w_a, w_b)

    def gate_block(slot, prev, aa, g):
        a = chunk_b * rows_per_step + aa
        rows = slice(aa * PEER_N_KEYS, (aa + 1) * PEER_N_KEYS)
        cols = slice(g * LANES, (g + 1) * LANES)
        gate = None
        zero = jnp.zeros((), BF16)
        pk = (2 * SUBLANES, LANES)
        blk = (PEER_N_KEYS // (2 * SUBLANES), 2 * SUBLANES, LANES)
        for hd in range(PEER_HEADS):
            pk_ref[2 * hd] = jnp.broadcast_to(th_ref[g, hd, pl.ds(a, 1), :], pk).astype(BF16)
            pk_ref[2 * hd + 1] = jnp.broadcast_to(e0_ref[g, hd, pl.ds(a, 1), :], pk).astype(BF16)
            th = pk_ref[2 * hd]
            e0 = pk_ref[2 * hd + 1]
            k1 = pltpu.bitcast(k1_ref[g, hd], BF16).reshape(blk)
            e1 = pltpu.bitcast(e1_ref[g, hd], BF16).reshape(blk)
            term = jnp.where(k1 >= th[None], e1 * e0[None], zero)
            gate = term if gate is None else gate + term
        sc = st_refs[prev][rows, cols]
        act = (0.5 * sc) * (1.0 + lax.erf(sc * (2.0 ** -0.5)))
        w_refs[slot][rows, cols] = (gate.reshape(PEER_N_KEYS, LANES).astype(F32) * act).astype(BF16)

    def step(slot, prev):
        def score_piece(p):
            tok = slice(p * half, (p + 1) * half)
            st_refs[slot][:, tok] = jnp.dot(u_ref[...], ht_ref[:, tok],
                                            preferred_element_type=F32)

        def value_piece(q):
            dr = slice(q * dblk, (q + 1) * dblk)
            acc_ref[dr, :] += jnp.dot(vt_ref[dr, :], w_refs[prev][...],
                                      preferred_element_type=F32)

        dblk = acc_ref.shape[0] // 4
        pieces = [(score_piece, 0), (value_piece, 0), (value_piece, 1),
                  (score_piece, 1), (value_piece, 2), (value_piece, 3)]
        blocks = [(aa, g) for g in range(n_groups) for aa in range(rows_per_step)]
        starts = [(k * len(blocks)) // len(pieces) for k in range(len(pieces))]
        for b, (aa, g) in enumerate(blocks):
            for (fn, arg), at in zip(pieces, starts):
                if at == b:
                    fn(arg)
            gate_block(slot, prev, aa, g)

    pl.when(s % 2 == 0)(lambda: step(0, 1))
    pl.when(s % 2 == 1)(lambda: step(1, 0))

    @pl.when((chunk_c == n_chunks - 1) & (s >= 2))
    def _():
        x2 = x1_ref[...] + mod_ref[0, 5:6, :] * acc_ref[...].T
        o_ref[...] = _rms(x2) * fn_ref[...]


def _peer_mix(h2t, x1, mod, k1, e1, e0, th, u, vt, fn, ct, ec, seq):
    d, t = h2t.shape
    rows = ec // PEER_N_KEYS
    n_chunks = PEER_EXPERTS // ec
    n_steps = (t // ct) * n_chunks

    def item(lag):
        def f(s):
            k = jnp.clip(s - lag, 0, n_steps - 1)
            return k // n_chunks, k % n_chunks
        return f

    a_item, b_item, c_item = item(0), item(1), item(2)
    sel = pl.BlockSpec((ct // LANES, PEER_HEADS, PEER_N_KEYS, LANES), lambda s: (b_item(s)[0], 0, 0, 0))
    sel_pk = pl.BlockSpec((ct // LANES, PEER_HEADS, PEER_N_KEYS // 2, LANES),
                          lambda s: (b_item(s)[0], 0, 0, 0))
    tok_c = pl.BlockSpec((ct, d), lambda s: (c_item(s)[0], 0))
    return pl.pallas_call(
        functools.partial(_peer_mix_kernel, rows_per_step=rows, n_chunks=n_chunks, n_steps=n_steps),
        grid=(n_steps + 2,),
        in_specs=[pl.BlockSpec((d, ct), lambda s: (0, a_item(s)[0])), tok_c,
                  pl.BlockSpec((1, N_MOD, d), lambda s: ((c_item(s)[0] * ct) // seq, 0, 0)),
                  sel_pk, sel_pk, sel, sel,
                  pl.BlockSpec((ec, d), lambda s: (a_item(s)[1], 0)),
                  pl.BlockSpec((d, ec), lambda s: (0, c_item(s)[1])),
                  pl.BlockSpec((1, d), lambda s: (0, 0))],
        out_specs=tok_c,
        out_shape=jax.ShapeDtypeStruct((t, d), F32),
        scratch_shapes=[pltpu.VMEM((d, ct), F32), pltpu.VMEM((ec, ct), F32), pltpu.VMEM((ec, ct), F32),
                        pltpu.VMEM((ec, ct), BF16), pltpu.VMEM((ec, ct), BF16),
                        pltpu.VMEM((2 * PEER_HEADS, 2 * SUBLANES, LANES), BF16)],
        compiler_params=_cparams(("arbitrary",)),
        name="peer_mix",
    )(h2t, x1, mod, k1, e1, e0, th, u, vt, fn)


def _rope_tables(seq):
    pos = jnp.arange(seq, dtype=F32)[:, None]

    def cs(dim):
        inv = 1.0 / (ROPE_THETA ** (jnp.arange(0, dim, 2, dtype=F32) / dim))
        ang = pos * inv[None, :]
        return jnp.cos(ang), jnp.sin(ang)

    ca, sa = cs(SWA_HEAD_DIM)
    cosa = jnp.concatenate([ca, ca, ca, ca], axis=1)
    sina = jnp.concatenate([-sa, sa, -sa, sa], axis=1)
    cm, sm = cs(MLA_ROPE_DIM)
    one = jnp.ones((seq, MLA_NOPE_DIM), F32)
    pad1 = jnp.ones((seq, LANES - MLA_QK_DIM), F32)
    cosm = jnp.concatenate([one, cm, cm, pad1], axis=1)
    sinm = jnp.concatenate([0.0 * one, -sm, sm, 0.0 * pad1], axis=1)
    return cosa, sina, cosm, sinm


def _tile(n, pref):
    t = min(n, pref)
    assert n % t == 0, (n, t)
    return t


def kernel(x, c, w_ada, b_ada, w_in, swa_sink, mla_q_norm, w_mla_q_up, mla_kv_norm, w_mla_kv_up,
           out_norm_swa, out_norm_mla, w_out, w_peer_query, peer_sub_keys, peer_expert_u,
           peer_expert_v, final_norm):
    bsz, seq, d = x.shape
    assert w_ada.shape[0] == 1 and seq % SWA_WINDOW == 0 and d == 1024
    qcols = np.concatenate([np.arange(h * SWA_HEAD_DIM, (h + 1) * SWA_HEAD_DIM) for h in SWA_HEAD_ORDER])
    cosa, sina, cosm, sinm = _rope_tables(seq)
    ts = _tile(seq, 512)
    tq_swa = _tile(seq, 256)
    tq_mla = _tile(seq, 256)
    ct_sel = _tile(bsz * seq, 256)
    ct_mix = _tile(seq, 512)

    for l in range(1):
        mod = _ada(c, w_ada[l], b_ada[l]).reshape(bsz, N_MOD, d)

        win = w_in[l]
        pad_cols = jnp.zeros((d, LANES - MLA_ROPE_DIM), win.dtype)
        win_p = jnp.concatenate([win[:, qcols], win[:, SWA_WIDTH:], pad_cols], axis=1).astype(BF16)
        wq_p = jnp.pad(w_mla_q_up[l].reshape(MLA_Q_RANK, MLA_HEADS, MLA_QK_DIM),
                       ((0, 0), (0, 0), (0, LANES - MLA_QK_DIM))).reshape(MLA_Q_RANK, MLA_HEADS * LANES).astype(BF16)
        wkv = w_mla_kv_up[l].reshape(MLA_KV_RANK, MLA_HEADS, MLA_NOPE_DIM + MLA_V_DIM)
        wk_p = jnp.pad(wkv[:, :, :MLA_NOPE_DIM], ((0, 0), (0, 0), (0, LANES - MLA_NOPE_DIM))
                       ).reshape(MLA_KV_RANK, MLA_HEADS * LANES).astype(BF16)
        wv_p = wkv[:, :, MLA_NOPE_DIM:].reshape(MLA_KV_RANK, MLA_WIDTH).astype(BF16)

        qa, ka, va, qm, km, vm = _inproj(
            x, mod, win_p, mla_q_norm[l].reshape(1, -1), wq_p, mla_kv_norm[l].reshape(1, -1), wk_p, wv_p,
            cosa, sina, cosm, sinm, ts)

        oa = _swa(swa_sink[l][np.array(SWA_HEAD_ORDER)], qa, ka, va, tq_swa)
        ob = _mla(qm, km, vm, tq_mla)

        wo = w_out[l]
        wo_p = jnp.concatenate([wo[qcols, :], wo[SWA_WIDTH:, :]], axis=0).astype(BF16)
        x1, h2t = _outproj(oa, ob, x, mod, out_norm_swa[l][qcols].reshape(1, -1),
                           out_norm_mla[l].reshape(1, -1), wo_p, ts)

        wqt = w_peer_query[l].T.astype(BF16)
        keys = peer_sub_keys[l].reshape(2 * PEER_HEADS, PEER_N_KEYS, PEER_HALF_DIM).astype(BF16)
        k1, e1, e0, th = _peer_sel(h2t, wqt, keys, ct_sel)

        u = peer_expert_u[l].astype(BF16)
        vt = peer_expert_v[l].T.astype(BF16)
        fn = final_norm.reshape(1, d)
        x = _peer_mix(h2t, x1.reshape(bsz * seq, d), mod, k1, e1, e0, th, u, vt, fn,
                      ct_mix, 512, seq).reshape(bsz, seq, d)
    return x
```
